```python
import math
import jax, jax.numpy as jnp
from jax import lax
import numpy as np

D_MODEL = 2048
BATCH = 4
SEQ = 2048
DEPTH = 4

SSD_HEADS = 32
SSD_HEAD_DIM = 64
SSD_INNER = SSD_HEADS * SSD_HEAD_DIM
SSD_STATE = 128
SSD_GROUPS = 4
SSD_CONV = 4
SSD_CHUNK = 256
SSD_CONV_DIM = SSD_INNER + 2 * SSD_GROUPS * SSD_STATE

SB_HEADS = 16
SB_HEAD_DIM = 128
SB_WIDTH = SB_HEADS * SB_HEAD_DIM

DIL_PAIRS = ((128, 1), (512, 4), (2048, 16))
N_DIL_GROUPS = len(DIL_PAIRS)
DIL_HEADS = 8
DIL_HEAD_DIM = 128
DIL_WIDTH = DIL_HEADS * DIL_HEAD_DIM

QUERY_BLOCK = 128
N_BRANCH = 3

N_EXPERTS = 16
N_EXPERT_GROUPS = 4
EXPERTS_PER_GROUP = N_EXPERTS // N_EXPERT_GROUPS
TOP_K = 2
D_EXPERT = 1024

DEEPNORM_ALPHA = (2 * DEPTH) ** 0.25
DEEPNORM_BETA = (8 * DEPTH) ** -0.25
EPS = 1e-5

IN_SIZES = (SSD_INNER, SSD_CONV_DIM, SSD_HEADS, 3 * SB_WIDTH,
            3 * N_DIL_GROUPS * DIL_WIDTH, N_BRANCH * D_MODEL)
N_IN = sum(IN_SIZES)

kernel_name = 'hybrid_ssd_stickbreak_dilated_moe'

F32 = jnp.float32


def split_columns(t, sizes):
    offs = np.cumsum(sizes)[:-1].tolist()
    return jnp.split(t, offs, axis=-1)


def layer_norm(x, g, b):
    xf = x.astype(F32)
    mu = jnp.mean(xf, axis=-1, keepdims=True)
    var = jnp.mean(jnp.square(xf - mu), axis=-1, keepdims=True)
    y = (xf - mu) * lax.rsqrt(var + EPS)
    return (y * g.astype(F32) + b.astype(F32)).astype(x.dtype)


def gated_rms_norm(y, z, g):
    yf = y.astype(F32) * jax.nn.silu(z.astype(F32))
    ms = jnp.mean(jnp.square(yf), axis=-1, keepdims=True)
    return (yf * lax.rsqrt(ms + EPS) * g.astype(F32)).astype(z.dtype)


def causal_depthwise_conv(x, w, b):
    k = w.shape[0]
    y = lax.conv_general_dilated(
        x, w[:, None, :].astype(x.dtype), window_strides=(1,), padding=[(k - 1, 0)],
        dimension_numbers=('NWC', 'WIO', 'NWC'), feature_group_count=x.shape[-1])
    return y + b


def ssd_chunked_scan(x, dt, a, bmat, cmat):
    bsz, s = x.shape[0], x.shape[1]
    chunk = math.gcd(s, SSD_CHUNK)
    nc = s // chunk
    r = SSD_HEADS // SSD_GROUPS
    xdt = (x.astype(F32) * dt[..., None]).reshape(bsz, nc, chunk, SSD_GROUPS, r, SSD_HEAD_DIM)
    a_dt = (dt * a).reshape(bsz, nc, chunk, SSD_GROUPS, r).transpose(0, 3, 4, 1, 2)
    bc = bmat.astype(F32).reshape(bsz, nc, chunk, SSD_GROUPS, SSD_STATE)
    cc = cmat.astype(F32).reshape(bsz, nc, chunk, SSD_GROUPS, SSD_STATE)
    a_cs = jnp.cumsum(a_dt, axis=-1)
    seg = a_cs[..., :, None] - a_cs[..., None, :]
    causal = jnp.tril(jnp.ones((chunk, chunk), dtype=bool))
    decay_in = jnp.exp(jnp.where(causal, seg, -jnp.inf))
    cb = jnp.einsum('bclgn,bcsgn->bgcls', cc, bc)
    y_diag = jnp.einsum('bgcls,bgrcls,bcsgrp->bclgrp', cb, decay_in, xdt)
    decay_to_end = jnp.exp(a_cs[..., -1:] - a_cs)
    states = jnp.einsum('bcsgn,bgrcs,bcsgrp->bcgrpn', bc, decay_to_end, xdt)
    chunk_decay = jnp.exp(a_cs[..., -1])

    def step(carry, inp):
        st, dec = inp
        return carry * dec[..., None, None] + st, carry

    init = jnp.zeros((bsz, SSD_GROUPS, r, SSD_HEAD_DIM, SSD_STATE), F32)
    _, prev = lax.scan(step, init, (jnp.moveaxis(states, 1, 0), jnp.moveaxis(chunk_decay, -1, 0)))
    prev = jnp.moveaxis(prev, 0, 1)
    y_off = jnp.einsum('bclgn,bcgrpn,bgrcl->bclgrp', cc, prev, jnp.exp(a_cs))
    return (y_diag + y_off).reshape(bsz, s, SSD_HEADS, SSD_HEAD_DIM)


def ssd_mixer(z, xbc, dt_raw, conv_w, conv_b, dt_bias, a_log, d_skip, norm_g):
    bsz, s = z.shape[0], z.shape[1]
    xbc = jax.nn.silu(causal_depthwise_conv(xbc, conv_w, conv_b))
    xs, bm, cm = split_columns(xbc, (SSD_INNER, SSD_GROUPS * SSD_STATE, SSD_GROUPS * SSD_STATE))
    xs = xs.reshape(bsz, s, SSD_HEADS, SSD_HEAD_DIM)
    bm = bm.reshape(bsz, s, SSD_GROUPS, SSD_STATE)
    cm = cm.reshape(bsz, s, SSD_GROUPS, SSD_STATE)
    dt = jax.nn.softplus(dt_raw.astype(F32) + dt_bias.astype(F32))
    a = -jnp.exp(a_log.astype(F32))
    y = ssd_chunked_scan(xs, dt, a, bm, cm) + d_skip.astype(F32)[:, None] * xs.astype(F32)
    return gated_rms_norm(y.reshape(bsz, s, SSD_INNER), z, norm_g)


def stick_breaking_attention(q, k, v):
    bsz, s, nh, hd = q.shape
    nb = s // QUERY_BLOCK
    scale = hd ** -0.5
    qb = q.astype(F32).reshape(bsz, nb, QUERY_BLOCK, nh, hd).transpose(1, 0, 3, 2, 4)
    kt = k.astype(F32).transpose(0, 2, 1, 3)
    vt = v.astype(F32).transpose(0, 2, 1, 3)
    key_pos = jnp.arange(s)

    def block(args):
        q_blk, t0 = args
        z = jnp.einsum('bhqd,bhkd->bhqk', q_blk, kt) * scale
        q_pos = t0 + jnp.arange(QUERY_BLOCK)
        mask = key_pos[None, :] < q_pos[:, None]
        log_one_minus = jnp.where(mask, -jax.nn.softplus(z), 0.0)
        suffix = lax.cumsum(log_one_minus, axis=3, reverse=True)
        log_w = jax.nn.log_sigmoid(z) + (suffix - log_one_minus)
        w = jnp.where(mask, jnp.exp(log_w), 0.0)
        return jnp.einsum('bhqk,bhkd->bhqd', w, vt)

    out = lax.map(block, (qb, jnp.arange(nb) * QUERY_BLOCK))
    return out.transpose(1, 0, 3, 2, 4).reshape(bsz, s, nh * hd).astype(v.dtype)


def dilated_group_attention(q, k, v, window, dilation):
    bsz, s, nh, hd = q.shape
    lc = s // dilation
    back = window // dilation
    bq = math.gcd(lc, QUERY_BLOCK)
    nb = lc // bq
    kl = bq + back

    def by_residue(t):
        return (t.astype(F32).reshape(bsz, lc, dilation, nh, hd)
                .transpose(0, 2, 1, 3, 4).reshape(bsz * dilation, lc, nh, hd))

    qs, ks, vs = by_residue(q), by_residue(k), by_residue(v)
    pad = ((0, 0), (back, 0), (0, 0), (0, 0))
    idx = np.arange(nb)[:, None] * bq + np.arange(kl)[None, :]
    kb = jnp.pad(ks, pad)[:, idx]
    vb = jnp.pad(vs, pad)[:, idx]
    qb = qs.reshape(bsz * dilation, nb, bq, nh, hd)
    sc = jnp.einsum('znqhd,znkhd->znqhk', qb, kb) * (hd ** -0.5)
    q_pos = np.arange(nb)[:, None] * bq + np.arange(bq)[None, :]
    k_pos = idx - back
    dist = q_pos[:, :, None] - k_pos[:, None, :]
    mask = (dist >= 0) & (dist <= back) & (k_pos[:, None, :] >= 0)
    sc = jnp.where(mask[None, :, :, None, :], sc, -jnp.inf)
    m = jnp.max(sc, axis=-1, keepdims=True)
    p = jnp.exp(sc - m)
    den = jnp.sum(p, axis=-1)
    o = jnp.einsum('znqhk,znkhd->znqhd', p, vb) / den[..., None]
    lse = m[..., 0] + jnp.log(den)
    o = o.reshape(bsz, dilation, lc, nh, hd).transpose(0, 2, 1, 3, 4).reshape(bsz, s, nh, hd)
    lse = lse.reshape(bsz, dilation, lc, nh).transpose(0, 2, 1, 3).reshape(bsz, s, nh)
    return o, lse


def dilated_mixture(qkv):
    bsz, s = qkv.shape[0], qkv.shape[1]
    qkv = qkv.reshape(bsz, s, N_DIL_GROUPS, 3, DIL_HEADS, DIL_HEAD_DIM)
    outs, lses = [], []
    for g, (window, dilation) in enumerate(DIL_PAIRS):
        o, l = dilated_group_attention(qkv[:, :, g, 0], qkv[:, :, g, 1], qkv[:, :, g, 2], window, dilation)
        outs.append(o)
        lses.append(l)
    wts = jax.nn.softmax(jnp.stack(lses, axis=0), axis=0)
    o = jnp.sum(wts[..., None] * jnp.stack(outs, axis=0), axis=0)
    return o.reshape(bsz, s, DIL_WIDTH).astype(qkv.dtype)


def grouped_moe(h, router_w, router_bias, w_gate, w_up, w_down):
    bsz, s, d = h.shape
    hf = h.reshape(bsz * s, d)
    aff = jax.nn.softmax((hf @ router_w).astype(F32), axis=-1)
    sel = aff + router_bias.astype(F32)
    grp = sel.reshape(-1, N_EXPERT_GROUPS, EXPERTS_PER_GROUP)
    grp_score = jnp.sum(lax.top_k(grp, TOP_K)[0], axis=-1)
    best = jnp.argmax(grp_score, axis=-1)
    in_grp = (jnp.arange(N_EXPERTS) // EXPERTS_PER_GROUP)[None, :] == best[:, None]
    _, e_idx = lax.top_k(jnp.where(in_grp, sel, -jnp.inf), TOP_K)
    w_sel = jnp.take_along_axis(aff, e_idx, axis=-1)
    w_sel = w_sel / jnp.sum(w_sel, axis=-1, keepdims=True)
    comb = jnp.sum(jax.nn.one_hot(e_idx, N_EXPERTS, dtype=F32) * w_sel[..., None], axis=1)
    y = jnp.zeros((bsz * s, d), F32)
    for e in range(N_EXPERTS):
        act = jax.nn.silu(hf @ w_gate[e]) * (hf @ w_up[e])
        y = y + comb[:, e:e + 1] * (act @ w_down[e]).astype(F32)
    return y.reshape(bsz, s, d).astype(h.dtype)


def setup_inputs(seed: int = 0) -> dict:
    key = jax.random.key(seed)
    ks = jax.random.split(key, 24)
    nrm = jax.random.normal
    L = DEPTH
    dt0 = jnp.exp(jax.random.uniform(ks[5], (L, SSD_HEADS), F32) * (math.log(0.1) - math.log(0.001)) + math.log(0.001))
    return {
        'x': nrm(ks[0], (BATCH, SEQ, D_MODEL), F32),
        'w_in': nrm(ks[1], (L, D_MODEL, N_IN), F32) * D_MODEL ** -0.5,
        'b_gate': 0.02 * nrm(ks[2], (L, N_BRANCH * D_MODEL), F32),
        'conv_w': nrm(ks[3], (L, SSD_CONV, SSD_CONV_DIM), F32) * SSD_CONV ** -0.5,
        'conv_b': 0.02 * nrm(ks[4], (L, SSD_CONV_DIM), F32),
        'dt_bias': dt0 + jnp.log(-jnp.expm1(-dt0)),
        'a_log': jnp.log(jax.random.uniform(ks[6], (L, SSD_HEADS), F32, 1.0, 16.0)),
        'd_skip': 1.0 + 0.02 * nrm(ks[7], (L, SSD_HEADS), F32),
        'ssm_norm_g': 1.0 + 0.02 * nrm(ks[8], (L, SSD_INNER), F32),
        'w_up_ssd': nrm(ks[9], (L, SSD_INNER, D_MODEL), F32) * SSD_INNER ** -0.5,
        'w_up_sb': nrm(ks[10], (L, SB_WIDTH, D_MODEL), F32) * SB_WIDTH ** -0.5,
        'w_up_dil': nrm(ks[11], (L, DIL_WIDTH, D_MODEL), F32) * DIL_WIDTH ** -0.5,
        'w_out': nrm(ks[12], (L, D_MODEL, D_MODEL), F32) * (D_MODEL ** -0.5 * DEEPNORM_BETA),
        'ln1_g': 1.0 + 0.02 * nrm(ks[13], (L, D_MODEL), F32),
        'ln1_b': 0.02 * nrm(ks[14], (L, D_MODEL), F32),
        'router_w': nrm(ks[15], (D_MODEL, N_EXPERTS), F32) * D_MODEL ** -0.5,
        'router_bias': 0.01 * nrm(ks[16], (N_EXPERTS,), F32),
        'w_gate_e': nrm(ks[17], (L, N_EXPERTS, D_MODEL, D_EXPERT), F32) * D_MODEL ** -0.5,
        'w_up_e': nrm(ks[18], (L, N_EXPERTS, D_MODEL, D_EXPERT), F32) * D_MODEL ** -0.5,
        'w_down_e': nrm(ks[19], (L, N_EXPERTS, D_EXPERT, D_MODEL), F32) * (D_EXPERT ** -0.5 * DEEPNORM_BETA),
        'ln2_g': 1.0 + 0.02 * nrm(ks[20], (L, D_MODEL), F32),
        'ln2_b': 0.02 * nrm(ks[21], (L, D_MODEL), F32),
    }


def reference(x, w_in, b_gate, conv_w, conv_b, dt_bias, a_log, d_skip, ssm_norm_g,
              w_up_ssd, w_up_sb, w_up_dil, w_out, ln1_g, ln1_b, router_w, router_bias,
              w_gate_e, w_up_e, w_down_e, ln2_g, ln2_b):
    bsz, s = x.shape[0], x.shape[1]
    h = x
    for l in range(DEPTH):
        proj = h @ w_in[l]
        z, xbc, dt_raw, sb_qkv, dil_qkv, gate_pre = split_columns(proj, IN_SIZES)
        y_ssd = ssd_mixer(z, xbc, dt_raw, conv_w[l], conv_b[l], dt_bias[l], a_log[l], d_skip[l], ssm_norm_g[l])
        q, k, v = jnp.split(sb_qkv, 3, axis=-1)
        hs = (bsz, s, SB_HEADS, SB_HEAD_DIM)
        y_sb = stick_breaking_attention(q.reshape(hs), k.reshape(hs), v.reshape(hs))
        y_dil = dilated_mixture(dil_qkv)
        gates = jax.nn.sigmoid(gate_pre.reshape(bsz, s, N_BRANCH, D_MODEL)
                               + b_gate[l].reshape(N_BRANCH, D_MODEL))
        merged = (gates[:, :, 0] * (y_ssd @ w_up_ssd[l])
                  + gates[:, :, 1] * (y_sb @ w_up_sb[l])
                  + gates[:, :, 2] * (y_dil @ w_up_dil[l]))
        h = layer_norm(DEEPNORM_ALPHA * h + merged @ w_out[l], ln1_g[l], ln1_b[l])
        ffn = grouped_moe(h, router_w, router_bias, w_gate_e[l], w_up_e[l], w_down_e[l])
        h = layer_norm(DEEPNORM_ALPHA * h + ffn, ln2_g[l], ln2_b[l])
    return h
```

```python
import functools

import jax
import jax.numpy as jnp
from jax import lax
from jax.experimental import pallas as pl
from jax.experimental.pallas import tpu as pltpu

F32 = jnp.float32
BF16 = jnp.bfloat16

D_MODEL = 2048
SSD_HEADS = 32
SSD_HEAD_DIM = 64
SSD_INNER = SSD_HEADS * SSD_HEAD_DIM
SSD_STATE = 128
SSD_GROUPS = 4
SSD_HEADS_PER_GROUP = SSD_HEADS // SSD_GROUPS
SSD_GROUP_WIDTH = SSD_HEADS_PER_GROUP * SSD_HEAD_DIM
SSD_CONV = 4
SSD_CHUNK = 256
SSD_BC = SSD_GROUPS * SSD_STATE
SB_HEADS = 16
SB_HEAD_DIM = 128
SB_WIDTH = SB_HEADS * SB_HEAD_DIM
DIL_PAIRS = ((128, 1), (512, 4), (2048, 16))
DIL_HEADS = 8
DIL_HEAD_DIM = 128
DIL_WIDTH = DIL_HEADS * DIL_HEAD_DIM
DIL_BLOCK = 128
N_BRANCH = 3
N_EXPERTS = 16
EXPERTS_PER_GROUP = 4
N_EXPERT_GROUPS = N_EXPERTS // EXPERTS_PER_GROUP
D_EXPERT = 1024
EPS = 1e-5
NEG = -1e30

_IN_DT0 = SSD_INNER + SSD_INNER + 2 * SSD_BC
_IN_DT1 = _IN_DT0 + SSD_HEADS
COL_Z = 0
COL_X = SSD_INNER
COL_B = COL_X + SSD_INNER
COL_C = COL_B + SSD_BC
COL_SB = COL_C + SSD_BC
COL_DIL = COL_SB + 3 * SB_WIDTH
COL_GATE = COL_DIL + 3 * len(DIL_PAIRS) * DIL_WIDTH
N_PROJ = COL_GATE + N_BRANCH * D_MODEL
DT_LANES = 128
DT_WIDTH = SSD_GROUPS * DT_LANES

V7X_LANES = 128
V7X_VMEM_LIMIT = 48 * 1024 * 1024
MM_TM, MM_TN = 1024, 1024
MERGE_TM, MERGE_TN = 512, 512
ROW_TM = 256
SB_BLOCK = 256
MOE_TM = 256


def _params(*sem):
    return pltpu.CompilerParams(dimension_semantics=sem, vmem_limit_bytes=V7X_VMEM_LIMIT)


def _dot(a, b):
    return jnp.dot(a, b, preferred_element_type=F32)


def _dot_nt(a, b):
    return lax.dot_general(a, b, (((1,), (1,)), ((), ())), preferred_element_type=F32)


def _split_dot(x, m, passes):
    acc = None
    r = x
    for p in range(passes):
        s = r.astype(BF16)
        d = _dot(s, m)
        acc = d if acc is None else acc + d
        if p + 1 < passes:
            r = r - s.astype(F32)
    return acc


def _split_dot_left(m, x, passes):
    acc = None
    r = x
    for p in range(passes):
        s = r.astype(BF16)
        d = _dot(m, s)
        acc = d if acc is None else acc + d
        if p + 1 < passes:
            r = r - s.astype(F32)
    return acc


def _softplus(x):
    return jnp.maximum(x, 0.0) + jnp.log1p(jnp.exp(-jnp.abs(x)))


def _silu(x):
    return x * jax.nn.sigmoid(x)


def _layer_norm(y, g, b):
    mu = jnp.mean(y, axis=-1, keepdims=True)
    yc = y - mu
    var = jnp.mean(yc * yc, axis=-1, keepdims=True)
    return yc * lax.rsqrt(var + EPS) * g + b


def _expander(rows, cols, width):
    h = lax.broadcasted_iota(jnp.int32, (rows, cols), 0)
    j = lax.broadcasted_iota(jnp.int32, (rows, cols), 1)
    return jnp.where(h == j // width, 1.0, 0.0).astype(BF16)


def _mm_kernel(a_ref, w_ref, o_ref):
    o_ref[...] = _dot(a_ref[...], w_ref[...]).astype(o_ref.dtype)


def _matmul(a, w, out_dtype, tm, tn):
    m, k = a.shape
    n = w.shape[1]
    tm, tn = min(tm, m), min(tn, n)
    return pl.pallas_call(
        _mm_kernel,
        out_shape=jax.ShapeDtypeStruct((m, n), out_dtype),
        grid=(m // tm, n // tn),
        in_specs=[pl.BlockSpec((tm, k), lambda i, j: (i, 0)),
                  pl.BlockSpec((k, tn), lambda i, j: (0, j))],
        out_specs=pl.BlockSpec((tm, tn), lambda i, j: (i, j)),
        compiler_params=_params("parallel", "parallel"),
        name="matmul",
    )(a, w)


def _ssd_kernel(z_ref, x_ref, b_ref, c_ref, dt_ref, cwx_ref, cwb_ref, cwc_ref, cbx_ref, cbb_ref, cbc_ref,
                dtb_ref, alog_ref, dskip_ref, g_ref, o_ref,
                state_ref, xpad_ref, bpad_ref, cpad_ref, xs_ref, bs_ref, cs_ref, y_ref,
                dts_ref, acs_ref, ecs_ref, dte_ref):
    chunk = x_ref.shape[1]
    pad = 8

    @pl.when(pl.program_id(1) == 0)
    def _():
        state_ref[...] = jnp.zeros_like(state_ref)
        xpad_ref[0:pad, :] = jnp.zeros((pad, xpad_ref.shape[1]), F32)
        bpad_ref[0:pad, :] = jnp.zeros((pad, bpad_ref.shape[1]), F32)
        cpad_ref[0:pad, :] = jnp.zeros((pad, cpad_ref.shape[1]), F32)

    def conv_silu(src_ref, pad_ref, cw_ref, cb_ref, dst_ref):
        pad_ref[pad:pad + chunk, :] = src_ref[0].astype(F32)
        acc = cb_ref[...]
        for k in range(SSD_CONV):
            off = pad - (SSD_CONV - 1) + k
            acc = acc + cw_ref[k:k + 1, :] * pad_ref[off:off + chunk, :]
        dst_ref[...] = _silu(acc)
        pad_ref[0:pad, :] = pad_ref[chunk:chunk + pad, :]

    conv_silu(x_ref, xpad_ref, cwx_ref, cbx_ref, xs_ref)
    conv_silu(b_ref, bpad_ref, cwb_ref, cbb_ref, bs_ref)
    conv_silu(c_ref, cpad_ref, cwc_ref, cbc_ref, cs_ref)

    dt = _softplus(dt_ref[0] + dtb_ref[...])
    a_dt = dt * (-jnp.exp(alog_ref[...]))
    row = lax.broadcasted_iota(jnp.int32, (chunk, chunk), 0)
    col = lax.broadcasted_iota(jnp.int32, (chunk, chunk), 1)
    causal = col <= row
    tri = jnp.where(causal, 1.0, 0.0).astype(BF16)
    a_cs = _split_dot_left(tri, a_dt, 3)
    last = a_cs[chunk - 1:chunk, :]
    dts_ref[...] = dt
    acs_ref[...] = a_cs
    ecs_ref[...] = jnp.exp(a_cs)
    dte_ref[...] = jnp.exp(last - a_cs)

    expand = _expander(DT_LANES, SSD_GROUP_WIDTH, SSD_HEAD_DIM)
    lane = lax.broadcasted_iota(jnp.int32, (chunk, V7X_LANES), 1)
    heads_per_tile = V7X_LANES // SSD_HEAD_DIM

    def group(g, carry):
        ch0 = pl.multiple_of(g * SSD_GROUP_WIDTH, SSD_GROUP_WIDTH)
        st0 = pl.multiple_of(g * SSD_STATE, SSD_STATE)
        dt0 = pl.multiple_of(g * DT_LANES, DT_LANES)
        xg = xs_ref[:, pl.ds(ch0, SSD_GROUP_WIDTH)]
        bg = bs_ref[:, pl.ds(st0, SSD_STATE)]
        cg = cs_ref[:, pl.ds(st0, SSD_STATE)].astype(BF16)
        csg = acs_ref[:, pl.ds(dt0, DT_LANES)]
        dt_x = _split_dot(dts_ref[:, pl.ds(dt0, DT_LANES)], expand, 2)
        ecs_x = _split_dot(ecs_ref[:, pl.ds(dt0, DT_LANES)], expand, 2)
        dte_x = _split_dot(dte_ref[:, pl.ds(dt0, DT_LANES)], expand, 2)
        xdt = xg * dt_x
        xdt_b = xdt.astype(BF16)
        cb = _dot_nt(cg, bg.astype(BF16))
        cs_t = csg.T
        prev = state_ref[g]
        y_off = _dot(cg, prev.astype(BF16)) * ecs_x
        for t in range(SSD_GROUP_WIDTH // V7X_LANES):
            lanes = slice(t * V7X_LANES, (t + 1) * V7X_LANES)
            xdt_t = xdt_b[:, lanes]
            y_t = None
            for u in range(heads_per_tile):
                r = t * heads_per_tile + u
                seg = csg[:, r:r + 1] - cs_t[r:r + 1, :]
                m = (cb * jnp.exp(jnp.where(causal, seg, NEG))).astype(BF16)
                y_r = _dot(m, xdt_t)
                y_t = y_r if y_t is None else jnp.where(lane // SSD_HEAD_DIM == u, y_r, y_t)
            c0 = pl.multiple_of(ch0 + t * V7X_LANES, V7X_LANES)
            y_ref[:, pl.ds(c0, V7X_LANES)] = (y_t + y_off[:, lanes]
                                              + dskip_ref[:, pl.ds(c0, V7X_LANES)] * xg[:, lanes])
        new = ecs_x[chunk - 1:chunk, :] * prev + _dot(bg.T.astype(BF16), (xdt * dte_x).astype(BF16))
        state_ref[g] = new
        return carry

    lax.fori_loop(0, SSD_GROUPS, group, 0)

    yf = y_ref[...] * _silu(z_ref[0].astype(F32))
    ms = jnp.mean(yf * yf, axis=-1, keepdims=True)
    o_ref[0] = (yf * lax.rsqrt(ms + EPS) * g_ref[...]).astype(o_ref.dtype)


def _ssd(proj, dt_raw, conv_w, conv_b, dt_bias, a_log, d_skip, norm_g):
    bsz, seq, _ = proj.shape
    chunk = min(SSD_CHUNK, seq)
    nc = seq // chunk

    def spread_heads(v):
        v = v.astype(F32).reshape(SSD_GROUPS, SSD_HEADS_PER_GROUP)
        return jnp.pad(v, ((0, 0), (0, DT_LANES - SSD_HEADS_PER_GROUP))).reshape(1, DT_WIDTH)

    cw = conv_w.astype(F32)
    cbias = conv_b.astype(F32).reshape(1, -1)
    x1, b1 = SSD_INNER, SSD_INNER + SSD_BC
    consts = [cw[:, :x1], cw[:, x1:b1], cw[:, b1:], cbias[:, :x1], cbias[:, x1:b1], cbias[:, b1:],
              spread_heads(dt_bias), spread_heads(a_log),
              jnp.repeat(d_skip.astype(F32), SSD_HEAD_DIM).reshape(1, SSD_INNER),
              norm_g.astype(F32).reshape(1, SSD_INNER)]

    def col_block(width, col):
        return pl.BlockSpec((1, chunk, width), lambda b, c: (b, c, col // width))

    def whole(a):
        return pl.BlockSpec(a.shape, lambda b, c: (0, 0))

    return pl.pallas_call(
        _ssd_kernel,
        out_shape=jax.ShapeDtypeStruct((bsz, seq, SSD_INNER), BF16),
        grid=(bsz, nc),
        in_specs=[col_block(SSD_INNER, COL_Z), col_block(SSD_INNER, COL_X),
                  col_block(SSD_BC, COL_B), col_block(SSD_BC, COL_C),
                  pl.BlockSpec((1, chunk, DT_WIDTH), lambda b, c: (b, c, 0))] + [whole(a) for a in consts],
        out_specs=pl.BlockSpec((1, chunk, SSD_INNER), lambda b, c: (b, c, 0)),
        scratch_shapes=[pltpu.VMEM((SSD_GROUPS, SSD_STATE, SSD_GROUP_WIDTH), F32),
                        pltpu.VMEM((chunk + 8, SSD_INNER), F32),
                        pltpu.VMEM((chunk + 8, SSD_BC), F32),
                        pltpu.VMEM((chunk + 8, SSD_BC), F32),
                        pltpu.VMEM((chunk, SSD_INNER), F32),
                        pltpu.VMEM((chunk, SSD_BC), F32),
                        pltpu.VMEM((chunk, SSD_BC), F32),
                        pltpu.VMEM((chunk, SSD_INNER), F32),
                        pltpu.VMEM((chunk, DT_WIDTH), F32),
                        pltpu.VMEM((chunk, DT_WIDTH), F32),
                        pltpu.VMEM((chunk, DT_WIDTH), F32),
                        pltpu.VMEM((chunk, DT_WIDTH), F32)],
        compiler_params=_params("parallel", "arbitrary"),
        name="ssd",
    )(proj, proj, proj, proj, dt_raw, *consts)


def _sb_kernel(q_ref, k_ref, v_ref, o_ref, *, blk, scale):
    qi = pl.program_id(2)
    q = q_ref[0]
    row = lax.broadcasted_iota(jnp.int32, (blk, blk), 0)
    col = lax.broadcasted_iota(jnp.int32, (blk, blk), 1)
    upper = jnp.where(row >= col, 1.0, 0.0).astype(BF16)
    upper2 = jnp.concatenate([upper, upper], axis=0)

    def body(j, carry):
        r_sum, acc = carry
        kb = qi - j
        start = pl.multiple_of(kb * blk, blk)
        k = k_ref[0, pl.ds(start, blk), :]
        v = v_ref[0, pl.ds(start, blk), :]
        z = _dot_nt(q, k) * scale
        mask = col + kb * blk < row + qi * blk
        lom = jnp.where(mask, -_softplus(z), 0.0)
        hi = lom.astype(BF16)
        lo = (lom - hi.astype(F32)).astype(BF16)
        suf = _dot(jnp.concatenate([hi, lo], axis=1), upper2)
        w = jnp.where(mask, jnp.exp(z + suf + r_sum), 0.0)
        acc = acc + _dot(w.astype(BF16), v)
        return r_sum + suf[:, 0:1], acc

    init = (jnp.zeros((blk, 1), F32), jnp.zeros((blk, q.shape[1]), F32))
    _, acc = lax.fori_loop(0, qi + 1, body, init)
    o_ref[0] = acc.astype(o_ref.dtype)


def _stick_breaking(proj):
    bsz, seq, _ = proj.shape
    blk = min(SB_BLOCK, seq)
    qc, kc, vc = (COL_SB + i * SB_WIDTH for i in range(3))
    hd = SB_HEAD_DIM
    return pl.pallas_call(
        functools.partial(_sb_kernel, blk=blk, scale=hd ** -0.5),
        out_shape=jax.ShapeDtypeStruct((bsz, seq, SB_WIDTH), BF16),
        grid=(bsz, SB_HEADS, seq // blk),
        in_specs=[pl.BlockSpec((1, blk, hd), lambda b, h, i: (b, i, qc // hd + h)),
                  pl.BlockSpec((1, seq, hd), lambda b, h, i: (b, 0, kc // hd + h)),
                  pl.BlockSpec((1, seq, hd), lambda b, h, i: (b, 0, vc // hd + h))],
        out_specs=pl.BlockSpec((1, blk, hd), lambda b, h, i: (b, i, h)),
        compiler_params=_params("parallel", "parallel", "arbitrary"),
        name="stick_breaking",
    )(proj, proj, proj)


def _dil_kernel(q_ref, kp_ref, kc_ref, vp_ref, vc_ref, o_ref, lse_ref, *, scale):
    bq = DIL_BLOCK
    blk = pl.program_id(2)
    i = lax.broadcasted_iota(jnp.int32, (bq, 2 * bq), 0)
    m = lax.broadcasted_iota(jnp.int32, (bq, 2 * bq), 1)
    dist = i - m + bq
    first = jnp.where(blk > 0, 0, bq)
    valid = (dist >= 0) & (dist <= bq) & (m >= first)
    lane = lax.broadcasted_iota(jnp.int32, (bq, V7X_LANES), 1)
    lse_all = jnp.zeros((bq, V7X_LANES), F32)
    for h in range(DIL_HEADS):
        sl = slice(h * DIL_HEAD_DIM, (h + 1) * DIL_HEAD_DIM)
        k = jnp.concatenate([kp_ref[0, :, sl], kc_ref[0, :, sl]], axis=0)
        v = jnp.concatenate([vp_ref[0, :, sl], vc_ref[0, :, sl]], axis=0)
        s = jnp.where(valid, _dot_nt(q_ref[0, :, sl], k) * scale, NEG)
        mx = jnp.max(s, axis=-1, keepdims=True)
        p = jnp.exp(s - mx)
        den = jnp.sum(p, axis=-1, keepdims=True)
        o_ref[0, :, sl] = (_dot(p.astype(BF16), v) / den).astype(o_ref.dtype)
        lse_all = jnp.where(lane == h, mx + jnp.log(den), lse_all)
    lse_ref[0] = lse_all


def _dilated_group(proj, group, window, dilation):
    bsz, seq, n_proj = proj.shape
    assert window // dilation == DIL_BLOCK and (seq // dilation) % DIL_BLOCK == 0
    lc = seq // dilation
    bq = DIL_BLOCK
    view = proj.reshape(bsz, lc, dilation * n_proj)
    w = DIL_WIDTH
    base = COL_DIL + group * 3 * w
    per_res = n_proj // w

    def cur(off):
        return pl.BlockSpec((1, bq, w), lambda b, c, i: (b, i, c * per_res + (base + off) // w))

    def prev(off):
        return pl.BlockSpec((1, bq, w), lambda b, c, i: (b, jnp.maximum(i - 1, 0), c * per_res + (base + off) // w))

    o, lse = pl.pallas_call(
        functools.partial(_dil_kernel, scale=DIL_HEAD_DIM ** -0.5),
        out_shape=(jax.ShapeDtypeStruct((bsz, lc, dilation * w), BF16),
                   jax.ShapeDtypeStruct((bsz, lc, dilation * V7X_LANES), F32)),
        grid=(bsz, dilation, lc // bq),
        in_specs=[cur(0), prev(w), cur(w), prev(2 * w), cur(2 * w)],
        out_specs=(pl.BlockSpec((1, bq, w), lambda b, c, i: (b, i, c)),
                   pl.BlockSpec((1, bq, V7X_LANES), lambda b, c, i: (b, i, c))),
        compiler_params=_params("parallel", "parallel", "arbitrary"),
        name="dilated_attention",
    )(view, view, view, view, view)
    return o.reshape(bsz * seq, w), lse.reshape(bsz * seq, V7X_LANES)


def _dilmix_kernel(o0_ref, o1_ref, o2_ref, l0_ref, l1_ref, l2_ref, out_ref):
    ls = [l0_ref[...], l1_ref[...], l2_ref[...]]
    mx = jnp.maximum(jnp.maximum(ls[0], ls[1]), ls[2])
    es = [jnp.exp(l - mx) for l in ls]
    inv = 1.0 / (es[0] + es[1] + es[2])
    expand = _expander(V7X_LANES, DIL_WIDTH, DIL_HEAD_DIM)
    acc = None
    for e, o_ref in zip(es, (o0_ref, o1_ref, o2_ref)):
        term = _split_dot(e * inv, expand, 2) * o_ref[...].astype(F32)
        acc = term if acc is None else acc + term
    out_ref[...] = acc.astype(out_ref.dtype)


def _dilated_mixture(proj):
    outs, lses = zip(*[_dilated_group(proj, g, w, d) for g, (w, d) in enumerate(DIL_PAIRS)])
    t = outs[0].shape[0]
    tm = min(ROW_TM, t)
    o_spec = pl.BlockSpec((tm, DIL_WIDTH), lambda i: (i, 0))
    l_spec = pl.BlockSpec((tm, V7X_LANES), lambda i: (i, 0))
    return pl.pallas_call(
        _dilmix_kernel,
        out_shape=jax.ShapeDtypeStruct((t, DIL_WIDTH), BF16),
        grid=(t // tm,),
        in_specs=[o_spec] * 3 + [l_spec] * 3,
        out_specs=o_spec,
        compiler_params=_params("parallel"),
        name="dilated_mix",
    )(*outs, *lses)


def _merge_kernel(ya_ref, yb_ref, yc_ref, ua_ref, ub_ref, uc_ref, ga_ref, gb_ref, gc_ref, bg_ref, o_ref):
    acc = None
    for i, (y_ref, u_ref, g_ref) in enumerate(((ya_ref, ua_ref, ga_ref), (yb_ref, ub_ref, gb_ref),
                                               (yc_ref, uc_ref, gc_ref))):
        gate = jax.nn.sigmoid(g_ref[...].astype(F32) + bg_ref[i:i + 1, :])
        term = gate * _dot(y_ref[...], u_ref[...])
        acc = term if acc is None else acc + term
    o_ref[...] = acc.astype(o_ref.dtype)


def _merge_up(y_ssd, y_sb, y_dil, u_ssd, u_sb, u_dil, proj2d, b_gate):
    t = y_ssd.shape[0]
    tm, tn = min(MERGE_TM, t), MERGE_TN

    def rows(a):
        return pl.BlockSpec((tm, a.shape[1]), lambda i, j: (i, 0))

    def cols(a):
        return pl.BlockSpec((a.shape[0], tn), lambda i, j: (0, j))

    def gate(k):
        return pl.BlockSpec((tm, tn), lambda i, j: (i, (COL_GATE + k * D_MODEL) // tn + j))

    return pl.pallas_call(
        _merge_kernel,
        out_shape=jax.ShapeDtypeStruct((t, D_MODEL), BF16),
        grid=(t // tm, D_MODEL // tn),
        in_specs=[rows(y_ssd), rows(y_sb), rows(y_dil), cols(u_ssd), cols(u_sb), cols(u_dil),
                  gate(0), gate(1), gate(2), pl.BlockSpec((N_BRANCH, tn), lambda i, j: (0, j))],
        out_specs=pl.BlockSpec((tm, tn), lambda i, j: (i, j)),
        compiler_params=_params("parallel", "parallel"),
        name="merge_up",
    )(y_ssd, y_sb, y_dil, u_ssd, u_sb, u_dil, proj2d, proj2d, proj2d, b_gate)


def _outln_kernel(m_ref, w_ref, h_ref, g_ref, b_ref, of_ref, ob_ref, *, alpha):
    y = alpha * h_ref[...] + _dot(m_ref[...], w_ref[...])
    out = _layer_norm(y, g_ref[...], b_ref[...])
    of_ref[...] = out
    ob_ref[...] = out.astype(BF16)


def _out_ln(merged, w_out, h, g, b, alpha):
    t, d = h.shape
    tm = min(ROW_TM, t)
    row = pl.BlockSpec((tm, d), lambda i: (i, 0))
    vec = pl.BlockSpec((1, d), lambda i: (0, 0))
    return pl.pallas_call(
        functools.partial(_outln_kernel, alpha=alpha),
        out_shape=(jax.ShapeDtypeStruct((t, d), F32), jax.ShapeDtypeStruct((t, d), BF16)),
        grid=(t // tm,),
        in_specs=[row, pl.BlockSpec((d, d), lambda i: (0, 0)), row, vec, vec],
        out_specs=(row, row),
        compiler_params=_params("parallel"),
        name="out_ln",
    )(merged, w_out, h, g.reshape(1, d), b.reshape(1, d))


def _router_kernel(h_ref, rw_ref, rb_ref, e_ref, w_ref):
    logits = lax.dot_general(rw_ref[...], h_ref[...], (((1,), (1,)), ((), ())),
                             precision=lax.Precision.HIGHEST, preferred_element_type=F32)
    mx = jnp.max(logits, axis=0, keepdims=True)
    ex = jnp.exp(logits - mx)
    aff = ex / jnp.sum(ex, axis=0, keepdims=True)
    sel = aff + rb_ref[...]
    s = [sel[e:e + 1, :] for e in range(N_EXPERTS)]
    a = [aff[e:e + 1, :] for e in range(N_EXPERTS)]
    n = EXPERTS_PER_GROUP
    best_score, best = None, None
    for g in range(N_EXPERT_GROUPS):
        grp = s[g * n:(g + 1) * n]
        score = None
        for i in range(n):
            for j in range(i + 1, n):
                pair = grp[i] + grp[j]
                score = pair if score is None else jnp.maximum(score, pair)
        if best is None:
            best_score, best = score, jnp.zeros_like(score, dtype=jnp.int32)
        else:
            take = score > best_score
            best_score = jnp.where(take, score, best_score)
            best = jnp.where(take, g, best)
    sv, av = [], []
    for r in range(n):
        sr, ar = s[r], a[r]
        for g in range(1, N_EXPERT_GROUPS):
            sr = jnp.where(best == g, s[g * n + r], sr)
            ar = jnp.where(best == g, a[g * n + r], ar)
        sv.append(sr)
        av.append(ar)

    def arg_top(vals, skip):
        top_v, top_i, top_a = None, None, None
        for r in range(n):
            v = vals[r] if skip is None else jnp.where(skip == r, -jnp.inf, vals[r])
            if top_v is None:
                top_v, top_i, top_a = v, jnp.zeros_like(best), av[0]
            else:
                take = v > top_v
                top_v = jnp.where(take, v, top_v)
                top_i = jnp.where(take, r, top_i)
                top_a = jnp.where(take, av[r], top_a)
        return top_i, top_a

    i1, a1 = arg_top(sv, None)
    i2, a2 = arg_top(sv, i1)
    tot = a1 + a2
    e_ref[0:1, :] = best * n + i1
    e_ref[1:2, :] = best * n + i2
    w_ref[0:1, :] = a1 / tot
    w_ref[1:2, :] = a2 / tot


def _router(h, router_w, router_bias):
    t, d = h.shape
    tm = min(ROW_TM, t)
    return pl.pallas_call(
        _router_kernel,
        out_shape=(jax.ShapeDtypeStruct((2, t), jnp.int32), jax.ShapeDtypeStruct((2, t), F32)),
        grid=(t // tm,),
        in_specs=[pl.BlockSpec((tm, d), lambda i: (i, 0)),
                  pl.BlockSpec((N_EXPERTS, d), lambda i: (0, 0)),
                  pl.BlockSpec((N_EXPERTS, 1), lambda i: (0, 0))],
        out_specs=(pl.BlockSpec((2, tm), lambda i: (0, i)), pl.BlockSpec((2, tm), lambda i: (0, i))),
        compiler_params=_params("parallel"),
        name="router",
    )(h, router_w.T.astype(F32), router_bias.astype(F32).reshape(N_EXPERTS, 1))


def _moe_kernel(te_ref, nu_ref, x_ref, wg_ref, wu_ref, wd_ref, o_ref):
    @pl.when(pl.program_id(0) < nu_ref[0])
    def _():
        x = x_ref[...]
        act = _silu(_dot(x, wg_ref[0])) * _dot(x, wu_ref[0])
        o_ref[...] = _dot(act.astype(BF16), wd_ref[0]).astype(o_ref.dtype)


def _route_plan(eidx, tm):
    t = eidx.shape[1]
    e = eidx.T.reshape(-1)
    one_hot = (e[:, None] == jnp.arange(N_EXPERTS, dtype=jnp.int32)[None, :]).astype(jnp.int32)
    csum = jnp.cumsum(one_hot, axis=0)
    rank = jnp.take_along_axis(csum, e[:, None], axis=1)[:, 0] - 1
    counts = csum[-1]
    padded = (counts + tm - 1) // tm * tm
    ends = jnp.cumsum(padded)
    pos = (ends - padded)[e] + rank
    n_slots = 2 * t + N_EXPERTS * tm
    n_tiles = n_slots // tm
    src = jnp.zeros((n_slots,), jnp.int32).at[pos].set(jnp.arange(2 * t, dtype=jnp.int32) // 2)
    n_used = (ends[-1] // tm).astype(jnp.int32)
    tile = jnp.arange(n_tiles, dtype=jnp.int32)
    owner = jnp.searchsorted(ends, jnp.minimum(tile, n_used - 1) * tm, side="right").astype(jnp.int32)
    return src, pos.reshape(t, 2), jnp.minimum(owner, N_EXPERTS - 1), n_used.reshape(1)


def _moe_experts(x_sorted, tile_expert, n_used, w_gate, w_up, w_down):
    n_slots, d = x_sorted.shape
    tm = MOE_TM
    de = w_gate.shape[2]

    def tile_rows(i, te, nu):
        return (jnp.minimum(i, nu[0] - 1), 0)

    return pl.pallas_call(
        _moe_kernel,
        out_shape=jax.ShapeDtypeStruct((n_slots, d), BF16),
        grid_spec=pltpu.PrefetchScalarGridSpec(
            num_scalar_prefetch=2,
            grid=(n_slots // tm,),
            in_specs=[pl.BlockSpec((tm, d), tile_rows),
                      pl.BlockSpec((1, d, de), lambda i, te, nu: (te[i], 0, 0)),
                      pl.BlockSpec((1, d, de), lambda i, te, nu: (te[i], 0, 0)),
                      pl.BlockSpec((1, de, d), lambda i, te, nu: (te[i], 0, 0))],
            out_specs=pl.BlockSpec((tm, d), tile_rows)),
        compiler_params=_params("arbitrary"),
        name="moe_experts",
    )(tile_expert, n_used, x_sorted, w_gate, w_up, w_down)


def _ln2_kernel(h_ref, r1_ref, r2_ref, w_ref, g_ref, b_ref, of_ref, ob_ref, *, alpha):
    w = w_ref[...]
    y = alpha * h_ref[...] + w[:, 0:1] * r1_ref[...].astype(F32) + w[:, 1:2] * r2_ref[...].astype(F32)
    out = _layer_norm(y, g_ref[...], b_ref[...])
    of_ref[...] = out
    ob_ref[...] = out.astype(BF16)


def _combine_ln(h, r1, r2, wts, g, b, alpha):
    t, d = h.shape
    tm = min(ROW_TM, t)
    row = pl.BlockSpec((tm, d), lambda i: (i, 0))
    vec = pl.BlockSpec((1, d), lambda i: (0, 0))
    return pl.pallas_call(
        functools.partial(_ln2_kernel, alpha=alpha),
        out_shape=(jax.ShapeDtypeStruct((t, d), F32), jax.ShapeDtypeStruct((t, d), BF16)),
        grid=(t // tm,),
        in_specs=[row, row, row, pl.BlockSpec((tm, 2), lambda i: (i, 0)), vec, vec],
        out_specs=(row, row),
        compiler_params=_params("parallel"),
        name="combine_ln",
    )(h, r1, r2, wts, g.reshape(1, d), b.reshape(1, d))


def _layer(h, h_b, bsz, seq, alpha, w_in, b_gate, conv_w, conv_b, dt_bias, a_log, d_skip, ssm_norm_g,
           w_up_ssd, w_up_sb, w_up_dil, w_out, ln1_g, ln1_b, router_w, router_bias,
           w_gate_e, w_up_e, w_down_e, ln2_g, ln2_b):
    t = bsz * seq
    w_main = jnp.concatenate([w_in[:, :_IN_DT0], w_in[:, _IN_DT1:]], axis=1).astype(BF16)
    w_dt = w_in[:, _IN_DT0:_IN_DT1].reshape(D_MODEL, SSD_GROUPS, SSD_HEADS_PER_GROUP)
    w_dt = jnp.pad(w_dt, ((0, 0), (0, 0), (0, DT_LANES - SSD_HEADS_PER_GROUP))).reshape(D_MODEL, DT_WIDTH)
    proj2d = _matmul(h_b, w_main, BF16, MM_TM, MM_TN)
    dt_raw = _matmul(h_b, w_dt.astype(BF16), F32, MM_TM, DT_WIDTH)
    proj = proj2d.reshape(bsz, seq, N_PROJ)

    y_ssd = _ssd(proj, dt_raw.reshape(bsz, seq, DT_WIDTH), conv_w, conv_b, dt_bias, a_log, d_skip, ssm_norm_g)
    y_sb = _stick_breaking(proj)
    y_dil = _dilated_mixture(proj)

    merged = _merge_up(y_ssd.reshape(t, SSD_INNER), y_sb.reshape(t, SB_WIDTH), y_dil,
                       w_up_ssd.astype(BF16), w_up_sb.astype(BF16), w_up_dil.astype(BF16),
                       proj2d, b_gate.astype(F32).reshape(N_BRANCH, D_MODEL))
    h1, h1_b = _out_ln(merged, w_out.astype(BF16), h, ln1_g, ln1_b, alpha)

    eidx, wts = _router(h1, router_w, router_bias)
    src, pos, tile_expert, n_used = _route_plan(eidx, MOE_TM)
    x_sorted = jnp.take(h1_b, src, axis=0)
    y_sorted = _moe_experts(x_sorted, tile_expert, n_used,
                            w_gate_e.astype(BF16), w_up_e.astype(BF16), w_down_e.astype(BF16))
    r1 = jnp.take(y_sorted, pos[:, 0], axis=0)
    r2 = jnp.take(y_sorted, pos[:, 1], axis=0)
    return _combine_ln(h1, r1, r2, wts.T, ln2_g, ln2_b, alpha)


def kernel(x, w_in, b_gate, conv_w, conv_b, dt_bias, a_log, d_skip, ssm_norm_g, w_up_ssd, w_up_sb, w_up_dil,
           w_out, ln1_g, ln1_b, router_w, router_bias, w_gate_e, w_up_e, w_down_e, ln2_g, ln2_b):
    bsz, seq, d = x.shape
    depth = w_in.shape[0]
    alpha = (2 * depth) ** 0.25
    h = x.reshape(bsz * seq, d)
    h_b = h.astype(BF16)
    for l in range(depth):
        h, h_b = _layer(h, h_b, bsz, seq, alpha, w_in[l], b_gate[l], conv_w[l], conv_b[l], dt_bias[l], a_log[l],
                        d_skip[l], ssm_norm_g[l], w_up_ssd[l], w_up_sb[l], w_up_dil[l], w_out[l],
                        ln1_g[l], ln1_b[l], router_w, router_bias, w_gate_e[l], w_up_e[l], w_down_e[l],
                        ln2_g[l], ln2_b[l])
    return h.reshape(bsz, seq, d)
```

```python
import functools

import jax
import jax.numpy as jnp
from jax import lax
from jax.experimental import pallas as pl
from jax.experimental.pallas import tpu as pltpu

F32 = jnp.float32
BF16 = jnp.bfloat16

D_MODEL = 2048
SSD_HEADS = 32
SSD_HEAD_DIM = 64
SSD_INNER = SSD_HEADS * SSD_HEAD_DIM
SSD_STATE = 128
SSD_GROUPS = 4
SSD_HEADS_PER_GROUP = SSD_HEADS // SSD_GROUPS
SSD_GROUP_WIDTH = SSD_HEADS_PER_GROUP * SSD_HEAD_DIM
SSD_CONV = 4
SSD_CHUNK = 256
SSD_BC = SSD_GROUPS * SSD_STATE
SB_HEADS = 16
SB_HEAD_DIM = 128
SB_WIDTH = SB_HEADS * SB_HEAD_DIM
DIL_PAIRS = ((128, 1), (512, 4), (2048, 16))
DIL_HEADS = 8
DIL_HEAD_DIM = 128
DIL_WIDTH = DIL_HEADS * DIL_HEAD_DIM
DIL_BLOCK = 128
N_BRANCH = 3
N_EXPERTS = 16
EXPERTS_PER_GROUP = 4
N_EXPERT_GROUPS = N_EXPERTS // EXPERTS_PER_GROUP
D_EXPERT = 1024
EPS = 1e-5
NEG = -1e30
LOG2E = 1.4426950408889634
F32_SUBNORMAL_EXP2 = -128.0

_IN_DT0 = SSD_INNER + SSD_INNER + 2 * SSD_BC
_IN_DT1 = _IN_DT0 + SSD_HEADS
_IN_DIL0 = _IN_DT1 + 3 * SB_WIDTH
_IN_GATE0 = _IN_DIL0 + 3 * len(DIL_PAIRS) * DIL_WIDTH
COL_Z = 0
COL_X = SSD_INNER
COL_B = COL_X + SSD_INNER
COL_C = COL_B + SSD_BC
COL_SB = COL_C + SSD_BC
COL_DIL = COL_SB + 3 * SB_WIDTH
COL_GATE = COL_DIL + 3 * DIL_WIDTH
N_PROJ = COL_GATE + N_BRANCH * D_MODEL
DT_LANES = 128
DT_WIDTH = SSD_GROUPS * DT_LANES

V7X_LANES = 128
V7X_VMEM_LIMIT = 48 * 1024 * 1024
MM_TM, MM_TN = 1024, 1024
MERGE_TM, MERGE_TN = 512, 512
ROW_TM = 256
SB_BLOCK = 256
SB_HEADS_PER_STEP = 4
SB_EXP2_CLAMP = 64.0
MOE_TM = 256


def _params(*sem):
    return pltpu.CompilerParams(dimension_semantics=sem, vmem_limit_bytes=V7X_VMEM_LIMIT)


def _dot(a, b):
    return jnp.dot(a, b, preferred_element_type=F32)


def _dot_nt(a, b):
    return lax.dot_general(a, b, (((1,), (1,)), ((), ())), preferred_element_type=F32)


def _split_dot(x, m, passes):
    acc = None
    r = x
    for p in range(passes):
        s = r.astype(BF16)
        d = _dot(s, m)
        acc = d if acc is None else acc + d
        if p + 1 < passes:
            r = r - s.astype(F32)
    return acc


def _split_dot_left(m, x, passes):
    acc = None
    r = x
    for p in range(passes):
        s = r.astype(BF16)
        d = _dot(m, s)
        acc = d if acc is None else acc + d
        if p + 1 < passes:
            r = r - s.astype(F32)
    return acc


def _softplus(x):
    return jnp.maximum(x, 0.0) + jnp.log1p(jnp.exp(-jnp.abs(x)))


def _silu(x):
    return x * jax.nn.sigmoid(x)


def _layer_norm(y, g, b):
    mu = jnp.mean(y, axis=-1, keepdims=True)
    yc = y - mu
    var = jnp.mean(yc * yc, axis=-1, keepdims=True)
    return yc * lax.rsqrt(var + EPS) * g + b


def _pack_halves(x):
    n = x.shape[1] // 2
    bits = pltpu.bitcast(x.astype(BF16).astype(F32), jnp.uint32)
    return (bits[:, n:] & jnp.uint32(0xFFFF0000)) | (bits[:, :n] >> 16)


def _unpack_halves(p):
    lo = pltpu.bitcast(p << 16, F32)
    hi = pltpu.bitcast(p & jnp.uint32(0xFFFF0000), F32)
    return jnp.concatenate([lo, hi], axis=1)


def _start_row_gather(idx_ref, base, src_hbm, dst, sem):
    def issue(r, carry):
        pltpu.make_async_copy(src_hbm.at[pl.ds(idx_ref[base + r], 1)], dst.at[pl.ds(r, 1)], sem).start()
        return carry

    lax.fori_loop(0, dst.shape[0], issue, 0, unroll=8)


def _wait_row_gather(src_hbm, dst, sem):
    pltpu.make_async_copy(src_hbm.at[pl.ds(0, dst.shape[0])], dst, sem).wait()


def _expander(rows, cols, width):
    h = lax.broadcasted_iota(jnp.int32, (rows, cols), 0)
    j = lax.broadcasted_iota(jnp.int32, (rows, cols), 1)
    return jnp.where(h == j // width, 1.0, 0.0).astype(BF16)


def _mm_kernel(a_ref, w_ref, o_ref):
    o_ref[...] = _dot(a_ref[...], w_ref[...]).astype(o_ref.dtype)


def _matmul(a, w, out_dtype, tm, tn):
    m, k = a.shape
    n = w.shape[1]
    tm, tn = min(tm, m), min(tn, n)
    return pl.pallas_call(
        _mm_kernel,
        out_shape=jax.ShapeDtypeStruct((m, n), out_dtype),
        grid=(m // tm, n // tn),
        in_specs=[pl.BlockSpec((tm, k), lambda i, j: (i, 0)),
                  pl.BlockSpec((k, tn), lambda i, j: (0, j))],
        out_specs=pl.BlockSpec((tm, tn), lambda i, j: (i, j)),
        compiler_params=_params("parallel", "parallel"),
        name="matmul",
    )(a, w)


def _ssd_kernel(z_ref, x_ref, b_ref, c_ref, dt_ref, cwx_ref, cwb_ref, cwc_ref, cbx_ref, cbb_ref, cbc_ref,
                dtb_ref, alog_ref, dskip_ref, g_ref, o_ref,
                state_ref, xpad_ref, bpad_ref, cpad_ref, xs_ref, bs_ref, cs_ref, y_ref,
                dts_ref, acs_ref, ecs_ref, dte_ref):
    chunk = x_ref.shape[1]
    pad = 8

    @pl.when(pl.program_id(1) == 0)
    def _():
        state_ref[...] = jnp.zeros_like(state_ref)
        xpad_ref[0:pad, :] = jnp.zeros((pad, xpad_ref.shape[1]), F32)
        bpad_ref[0:pad, :] = jnp.zeros((pad, bpad_ref.shape[1]), F32)
        cpad_ref[0:pad, :] = jnp.zeros((pad, cpad_ref.shape[1]), F32)

    def conv_silu(src_ref, pad_ref, cw_ref, cb_ref, dst_ref):
        pad_ref[pad:pad + chunk, :] = src_ref[0].astype(F32)
        acc = cb_ref[...]
        for k in range(SSD_CONV):
            off = pad - (SSD_CONV - 1) + k
            acc = acc + cw_ref[k:k + 1, :] * pad_ref[off:off + chunk, :]
        dst_ref[...] = _silu(acc)
        pad_ref[0:pad, :] = pad_ref[chunk:chunk + pad, :]

    conv_silu(x_ref, xpad_ref, cwx_ref, cbx_ref, xs_ref)
    conv_silu(b_ref, bpad_ref, cwb_ref, cbb_ref, bs_ref)
    conv_silu(c_ref, cpad_ref, cwc_ref, cbc_ref, cs_ref)

    dt = _softplus(dt_ref[0] + dtb_ref[...])
    a_dt = dt * (-jnp.exp(alog_ref[...]))
    row = lax.broadcasted_iota(jnp.int32, (chunk, chunk), 0)
    col = lax.broadcasted_iota(jnp.int32, (chunk, chunk), 1)
    causal = col <= row
    tri = jnp.where(causal, 1.0, 0.0).astype(BF16)
    a_cs = _split_dot_left(tri, a_dt, 3)
    last = a_cs[chunk - 1:chunk, :]
    dts_ref[...] = dt
    acs_ref[...] = a_cs
    ecs_ref[...] = jnp.exp(a_cs)
    dte_ref[...] = jnp.exp(last - a_cs)

    expand = _expander(DT_LANES, SSD_GROUP_WIDTH, SSD_HEAD_DIM)
    lane = lax.broadcasted_iota(jnp.int32, (chunk, V7X_LANES), 1)
    heads_per_tile = V7X_LANES // SSD_HEAD_DIM

    def group(g, carry):
        ch0 = pl.multiple_of(g * SSD_GROUP_WIDTH, SSD_GROUP_WIDTH)
        st0 = pl.multiple_of(g * SSD_STATE, SSD_STATE)
        dt0 = pl.multiple_of(g * DT_LANES, DT_LANES)
        xg = xs_ref[:, pl.ds(ch0, SSD_GROUP_WIDTH)]
        bg = bs_ref[:, pl.ds(st0, SSD_STATE)]
        cg = cs_ref[:, pl.ds(st0, SSD_STATE)].astype(BF16)
        csg = acs_ref[:, pl.ds(dt0, DT_LANES)]
        dt_x = _split_dot(dts_ref[:, pl.ds(dt0, DT_LANES)], expand, 2)
        ecs_x = _split_dot(ecs_ref[:, pl.ds(dt0, DT_LANES)], expand, 2)
        dte_x = _split_dot(dte_ref[:, pl.ds(dt0, DT_LANES)], expand, 2)
        xdt = xg * dt_x
        xdt_b = xdt.astype(BF16)
        cb = _dot_nt(cg, bg.astype(BF16))
        cs_t = csg.T
        prev = state_ref[g]
        y_off = _dot(cg, prev.astype(BF16)) * ecs_x
        for t in range(SSD_GROUP_WIDTH // V7X_LANES):
            lanes = slice(t * V7X_LANES, (t + 1) * V7X_LANES)
            xdt_t = xdt_b[:, lanes]
            y_t = None
            for u in range(heads_per_tile):
                r = t * heads_per_tile + u
                seg = csg[:, r:r + 1] - cs_t[r:r + 1, :]
                m = (cb * jnp.exp(jnp.where(causal, seg, NEG))).astype(BF16)
                y_r = _dot(m, xdt_t)
                y_t = y_r if y_t is None else jnp.where(lane // SSD_HEAD_DIM == u, y_r, y_t)
            c0 = pl.multiple_of(ch0 + t * V7X_LANES, V7X_LANES)
            y_ref[:, pl.ds(c0, V7X_LANES)] = (y_t + y_off[:, lanes]
                                              + dskip_ref[:, pl.ds(c0, V7X_LANES)] * xg[:, lanes])
        new = ecs_x[chunk - 1:chunk, :] * prev + _dot(bg.T.astype(BF16), (xdt * dte_x).astype(BF16))
        state_ref[g] = new
        return carry

    lax.fori_loop(0, SSD_GROUPS, group, 0)

    yf = y_ref[...] * _silu(z_ref[0].astype(F32))
    ms = jnp.mean(yf * yf, axis=-1, keepdims=True)
    o_ref[0] = (yf * lax.rsqrt(ms + EPS) * g_ref[...]).astype(o_ref.dtype)


def _ssd(proj, dt_raw, conv_w, conv_b, dt_bias, a_log, d_skip, norm_g):
    bsz, seq, _ = proj.shape
    chunk = min(SSD_CHUNK, seq)
    nc = seq // chunk

    def spread_heads(v):
        v = v.astype(F32).reshape(SSD_GROUPS, SSD_HEADS_PER_GROUP)
        return jnp.pad(v, ((0, 0), (0, DT_LANES - SSD_HEADS_PER_GROUP))).reshape(1, DT_WIDTH)

    cw = conv_w.astype(F32)
    cbias = conv_b.astype(F32).reshape(1, -1)
    x1, b1 = SSD_INNER, SSD_INNER + SSD_BC
    consts = [cw[:, :x1], cw[:, x1:b1], cw[:, b1:], cbias[:, :x1], cbias[:, x1:b1], cbias[:, b1:],
              spread_heads(dt_bias), spread_heads(a_log),
              jnp.repeat(d_skip.astype(F32), SSD_HEAD_DIM).reshape(1, SSD_INNER),
              norm_g.astype(F32).reshape(1, SSD_INNER)]

    def col_block(width, col):
        return pl.BlockSpec((1, chunk, width), lambda b, c: (b, c, col // width))

    def whole(a):
        return pl.BlockSpec(a.shape, lambda b, c: (0, 0))

    return pl.pallas_call(
        _ssd_kernel,
        out_shape=jax.ShapeDtypeStruct((bsz, seq, SSD_INNER), BF16),
        grid=(bsz, nc),
        in_specs=[col_block(SSD_INNER, COL_Z), col_block(SSD_INNER, COL_X),
                  col_block(SSD_BC, COL_B), col_block(SSD_BC, COL_C),
                  pl.BlockSpec((1, chunk, DT_WIDTH), lambda b, c: (b, c, 0))] + [whole(a) for a in consts],
        out_specs=pl.BlockSpec((1, chunk, SSD_INNER), lambda b, c: (b, c, 0)),
        scratch_shapes=[pltpu.VMEM((SSD_GROUPS, SSD_STATE, SSD_GROUP_WIDTH), F32),
                        pltpu.VMEM((chunk + 8, SSD_INNER), F32),
                        pltpu.VMEM((chunk + 8, SSD_BC), F32),
                        pltpu.VMEM((chunk + 8, SSD_BC), F32),
                        pltpu.VMEM((chunk, SSD_INNER), F32),
                        pltpu.VMEM((chunk, SSD_BC), F32),
                        pltpu.VMEM((chunk, SSD_BC), F32),
                        pltpu.VMEM((chunk, SSD_INNER), F32),
                        pltpu.VMEM((chunk, DT_WIDTH), F32),
                        pltpu.VMEM((chunk, DT_WIDTH), F32),
                        pltpu.VMEM((chunk, DT_WIDTH), F32),
                        pltpu.VMEM((chunk, DT_WIDTH), F32)],
        compiler_params=_params("parallel", "arbitrary"),
        name="ssd",
    )(proj, proj, proj, proj, dt_raw, *consts)


def _sb_kernel(q_ref, k_ref, v_ref, o_ref, *, blk, scale2):
    qi = pl.program_id(2)
    hd = SB_HEAD_DIM
    row = lax.broadcasted_iota(jnp.int32, (blk, blk), 0)
    col = lax.broadcasted_iota(jnp.int32, (blk, blk), 1)
    strict = col < row
    upper = jnp.where(row >= col, 1.0, 0.0).astype(BF16)
    upper2 = jnp.concatenate([upper, upper], axis=0)

    lanes = [slice(h * hd, (h + 1) * hd) for h in range(q_ref.shape[2] // hd)]

    def blocks(kb, rs, diagonal):
        start = pl.multiple_of(kb * blk, blk)
        zs = [_dot_nt(q_ref[0, :, sl], k_ref[0, pl.ds(start, blk), sl]) * scale2 for sl in lanes]
        sufs = []
        for z in zs:
            sp = jnp.maximum(z, jnp.log2(1.0 + jnp.exp2(jnp.minimum(z, SB_EXP2_CLAMP))))
            if diagonal:
                sp = jnp.where(strict, sp, 0.0)
            hi = pltpu.bitcast(pltpu.bitcast(sp, jnp.uint32) & jnp.uint32(0xFFFF0000), F32)
            parts = jnp.concatenate([hi.astype(BF16), (sp - hi).astype(BF16)], axis=1)
            sufs.append(_dot(parts, upper2))
        new_rs = tuple(r - suf[:, 0:1] for r, suf in zip(rs, sufs))
        pvs = []
        for z, suf, r, sl in zip(zs, sufs, rs, lanes):
            w = jnp.exp2(z - suf + r)
            if diagonal:
                w = jnp.where(strict, w, 0.0)
            pvs.append(_dot(w.astype(BF16), v_ref[0, pl.ds(start, blk), sl]))
        return new_rs, pvs

    def largest(rs):
        m = rs[0]
        for r in rs[1:]:
            m = jnp.maximum(m, r)
        return jnp.max(m)

    rs, accs = blocks(qi, tuple(jnp.zeros((blk, 1), F32) for _ in lanes), True)

    def more(c):
        return (c[0] >= 0) & (c[1] > F32_SUBNORMAL_EXP2)

    def body(c):
        kb, _, rs, accs = c
        rs, pvs = blocks(kb, rs, False)
        return kb - 1, largest(rs), rs, tuple(a + pv for a, pv in zip(accs, pvs))

    _, _, _, accs = lax.while_loop(more, body, (qi - 1, largest(rs), rs, tuple(accs)))
    o_ref[0] = jnp.concatenate(accs, axis=1).astype(o_ref.dtype)


def _stick_breaking(proj):
    bsz, seq, _ = proj.shape
    blk = min(SB_BLOCK, seq)
    qc, kc, vc = (COL_SB + i * SB_WIDTH for i in range(3))
    w = SB_HEADS_PER_STEP * SB_HEAD_DIM
    return pl.pallas_call(
        functools.partial(_sb_kernel, blk=blk, scale2=SB_HEAD_DIM ** -0.5 * LOG2E),
        out_shape=jax.ShapeDtypeStruct((bsz, seq, SB_WIDTH), BF16),
        grid=(bsz, SB_WIDTH // w, seq // blk),
        in_specs=[pl.BlockSpec((1, blk, w), lambda b, h, i: (b, i, qc // w + h)),
                  pl.BlockSpec((1, seq, w), lambda b, h, i: (b, 0, kc // w + h)),
                  pl.BlockSpec((1, seq, w), lambda b, h, i: (b, 0, vc // w + h))],
        out_specs=pl.BlockSpec((1, blk, w), lambda b, h, i: (b, i, h)),
        compiler_params=_params("parallel", "parallel", "arbitrary"),
        name="stick_breaking",
    )(proj, proj, proj)


def _dil_kernel(q_ref, kp_ref, kc_ref, vp_ref, vc_ref, o_ref, lse_ref, *, scale):
    bq = DIL_BLOCK
    blk = pl.program_id(1)
    i = lax.broadcasted_iota(jnp.int32, (bq, 2 * bq), 0)
    m = lax.broadcasted_iota(jnp.int32, (bq, 2 * bq), 1)
    dist = i - m + bq
    first = jnp.where(blk > 0, 0, bq)
    valid = (dist >= 0) & (dist <= bq) & (m >= first)
    lane = lax.broadcasted_iota(jnp.int32, (bq, V7X_LANES), 1)
    lse_all = jnp.zeros((bq, V7X_LANES), F32)
    for h in range(DIL_HEADS):
        sl = slice(h * DIL_HEAD_DIM, (h + 1) * DIL_HEAD_DIM)
        k = jnp.concatenate([kp_ref[0, :, sl], kc_ref[0, :, sl]], axis=0)
        v = jnp.concatenate([vp_ref[0, :, sl], vc_ref[0, :, sl]], axis=0)
        s = jnp.where(valid, _dot_nt(q_ref[0, :, sl], k) * scale, NEG)
        mx = jnp.max(s, axis=-1, keepdims=True)
        p = jnp.exp(s - mx)
        den = jnp.sum(p, axis=-1, keepdims=True)
        o_ref[0, :, sl] = (_dot(p.astype(BF16), v) / den).astype(o_ref.dtype)
        lse_all = jnp.where(lane == h, mx + jnp.log(den), lse_all)
    lse_ref[0] = lse_all


def _band_attention(qkv, col):
    nz, lc, _ = qkv.shape
    bq = DIL_BLOCK
    w = DIL_WIDTH
    assert lc % bq == 0 and col % w == 0

    def cur(part):
        return pl.BlockSpec((1, bq, w), lambda z, i: (z, i, col // w + part))

    def prev(part):
        return pl.BlockSpec((1, bq, w), lambda z, i: (z, jnp.maximum(i - 1, 0), col // w + part))

    return pl.pallas_call(
        functools.partial(_dil_kernel, scale=DIL_HEAD_DIM ** -0.5),
        out_shape=(jax.ShapeDtypeStruct((nz, lc, w), BF16),
                   jax.ShapeDtypeStruct((nz, lc, V7X_LANES), F32)),
        grid=(nz, lc // bq),
        in_specs=[cur(0), prev(1), cur(1), prev(2), cur(2)],
        out_specs=(pl.BlockSpec((1, bq, w), lambda z, i: (z, i, 0)),
                   pl.BlockSpec((1, bq, V7X_LANES), lambda z, i: (z, i, 0))),
        compiler_params=_params("parallel", "arbitrary"),
        name="dilated_attention",
    )(qkv, qkv, qkv, qkv, qkv)


def _by_residue(a, bsz, seq, dilation):
    c = a.shape[-1]
    return a.reshape(bsz, seq // dilation, dilation, c).transpose(0, 2, 1, 3).reshape(bsz * seq, c)


def _from_residue(a, bsz, seq, dilation):
    c = a.shape[-1]
    return a.reshape(bsz, dilation, seq // dilation, c).transpose(0, 2, 1, 3).reshape(bsz * seq, c)


def _dilmix_kernel(o0_ref, o1_ref, o2_ref, l0_ref, l1_ref, l2_ref, out_ref):
    ls = [l0_ref[...], l1_ref[...], l2_ref[...]]
    mx = jnp.maximum(jnp.maximum(ls[0], ls[1]), ls[2])
    es = [jnp.exp(l - mx) for l in ls]
    inv = 1.0 / (es[0] + es[1] + es[2])
    expand = _expander(V7X_LANES, DIL_WIDTH, DIL_HEAD_DIM)
    acc = None
    for e, o_ref in zip(es, (o0_ref, o1_ref, o2_ref)):
        term = _split_dot(e * inv, expand, 2) * o_ref[...].astype(F32)
        acc = term if acc is None else acc + term
    out_ref[...] = acc.astype(out_ref.dtype)


def _dilated_mixture(outs, lses):
    t = outs[0].shape[0]
    tm = min(ROW_TM, t)
    o_spec = pl.BlockSpec((tm, DIL_WIDTH), lambda i: (i, 0))
    l_spec = pl.BlockSpec((tm, V7X_LANES), lambda i: (i, 0))
    return pl.pallas_call(
        _dilmix_kernel,
        out_shape=jax.ShapeDtypeStruct((t, DIL_WIDTH), BF16),
        grid=(t // tm,),
        in_specs=[o_spec] * 3 + [l_spec] * 3,
        out_specs=o_spec,
        compiler_params=_params("parallel"),
        name="dilated_mix",
    )(*outs, *lses)


def _merge_kernel(ya_ref, yb_ref, yc_ref, ua_ref, ub_ref, uc_ref, ga_ref, gb_ref, gc_ref, bg_ref, o_ref):
    acc = None
    for i, (y_ref, u_ref, g_ref) in enumerate(((ya_ref, ua_ref, ga_ref), (yb_ref, ub_ref, gb_ref),
                                               (yc_ref, uc_ref, gc_ref))):
        gate = jax.nn.sigmoid(g_ref[...].astype(F32) + bg_ref[i:i + 1, :])
        term = gate * _dot(y_ref[...], u_ref[...])
        acc = term if acc is None else acc + term
    o_ref[...] = acc.astype(o_ref.dtype)


def _merge_up(y_ssd, y_sb, y_dil, u_ssd, u_sb, u_dil, proj2d, b_gate):
    t = y_ssd.shape[0]
    tm, tn = min(MERGE_TM, t), MERGE_TN

    def rows(a):
        return pl.BlockSpec((tm, a.shape[1]), lambda i, j: (i, 0))

    def cols(a):
        return pl.BlockSpec((a.shape[0], tn), lambda i, j: (0, j))

    def gate(k):
        return pl.BlockSpec((tm, tn), lambda i, j: (i, (COL_GATE + k * D_MODEL) // tn + j))

    return pl.pallas_call(
        _merge_kernel,
        out_shape=jax.ShapeDtypeStruct((t, D_MODEL), BF16),
        grid=(t // tm, D_MODEL // tn),
        in_specs=[rows(y_ssd), rows(y_sb), rows(y_dil), cols(u_ssd), cols(u_sb), cols(u_dil),
                  gate(0), gate(1), gate(2), pl.BlockSpec((N_BRANCH, tn), lambda i, j: (0, j))],
        out_specs=pl.BlockSpec((tm, tn), lambda i, j: (i, j)),
        compiler_params=_params("parallel", "parallel"),
        name="merge_up",
    )(y_ssd, y_sb, y_dil, u_ssd, u_sb, u_dil, proj2d, proj2d, proj2d, b_gate)


def _outln_kernel(m_ref, w_ref, h_ref, g_ref, b_ref, of_ref, op_ref, *, alpha):
    y = alpha * h_ref[...] + _dot(m_ref[...], w_ref[...])
    out = _layer_norm(y, g_ref[...], b_ref[...])
    of_ref[...] = out
    op_ref[...] = _pack_halves(out)


def _out_ln(merged, w_out, h, g, b, alpha):
    t, d = h.shape
    tm = min(ROW_TM, t)
    row = pl.BlockSpec((tm, d), lambda i: (i, 0))
    vec = pl.BlockSpec((1, d), lambda i: (0, 0))
    return pl.pallas_call(
        functools.partial(_outln_kernel, alpha=alpha),
        out_shape=(jax.ShapeDtypeStruct((t, d), F32), jax.ShapeDtypeStruct((t, d // 2), jnp.uint32)),
        grid=(t // tm,),
        in_specs=[row, pl.BlockSpec((d, d), lambda i: (0, 0)), row, vec, vec],
        out_specs=(row, pl.BlockSpec((tm, d // 2), lambda i: (i, 0))),
        compiler_params=_params("parallel"),
        name="out_ln",
    )(merged, w_out, h, g.reshape(1, d), b.reshape(1, d))


def _router_kernel(h_ref, rw_ref, rb_ref, e_ref, w_ref):
    logits = lax.dot_general(rw_ref[...], h_ref[...], (((1,), (1,)), ((), ())),
                             precision=lax.Precision.HIGHEST, preferred_element_type=F32)
    mx = jnp.max(logits, axis=0, keepdims=True)
    ex = jnp.exp(logits - mx)
    aff = ex / jnp.sum(ex, axis=0, keepdims=True)
    sel = aff + rb_ref[...]
    s = [sel[e:e + 1, :] for e in range(N_EXPERTS)]
    a = [aff[e:e + 1, :] for e in range(N_EXPERTS)]
    n = EXPERTS_PER_GROUP
    best_score, best = None, None
    for g in range(N_EXPERT_GROUPS):
        grp = s[g * n:(g + 1) * n]
        score = None
        for i in range(n):
            for j in range(i + 1, n):
                pair = grp[i] + grp[j]
                score = pair if score is None else jnp.maximum(score, pair)
        if best is None:
            best_score, best = score, jnp.zeros_like(score, dtype=jnp.int32)
        else:
            take = score > best_score
            best_score = jnp.where(take, score, best_score)
            best = jnp.where(take, g, best)
    sv, av = [], []
    for r in range(n):
        sr, ar = s[r], a[r]
        for g in range(1, N_EXPERT_GROUPS):
            sr = jnp.where(best == g, s[g * n + r], sr)
            ar = jnp.where(best == g, a[g * n + r], ar)
        sv.append(sr)
        av.append(ar)

    def arg_top(vals, skip):
        top_v, top_i, top_a = None, None, None
        for r in range(n):
            v = vals[r] if skip is None else jnp.where(skip == r, -jnp.inf, vals[r])
            if top_v is None:
                top_v, top_i, top_a = v, jnp.zeros_like(best), av[0]
            else:
                take = v > top_v
                top_v = jnp.where(take, v, top_v)
                top_i = jnp.where(take, r, top_i)
                top_a = jnp.where(take, av[r], top_a)
        return top_i, top_a

    i1, a1 = arg_top(sv, None)
    i2, a2 = arg_top(sv, i1)
    tot = a1 + a2
    e_ref[0:1, :] = best * n + i1
    e_ref[1:2, :] = best * n + i2
    w_ref[0:1, :] = a1 / tot
    w_ref[1:2, :] = a2 / tot


def _router(h, router_w, router_bias):
    t, d = h.shape
    tm = min(ROW_TM, t)
    return pl.pallas_call(
        _router_kernel,
        out_shape=(jax.ShapeDtypeStruct((2, t), jnp.int32), jax.ShapeDtypeStruct((2, t), F32)),
        grid=(t // tm,),
        in_specs=[pl.BlockSpec((tm, d), lambda i: (i, 0)),
                  pl.BlockSpec((N_EXPERTS, d), lambda i: (0, 0)),
                  pl.BlockSpec((N_EXPERTS, 1), lambda i: (0, 0))],
        out_specs=(pl.BlockSpec((2, tm), lambda i: (0, i)), pl.BlockSpec((2, tm), lambda i: (0, i))),
        compiler_params=_params("parallel"),
        name="router",
    )(h, router_w.T.astype(F32), router_bias.astype(F32).reshape(N_EXPERTS, 1))


def _moe_kernel(te_ref, nu_ref, src_ref, x_hbm, wg_ref, wu_ref, wd_ref, o_ref, xbuf, sem):
    i = pl.program_id(0)
    n_used = nu_ref[0]
    tm = xbuf.shape[1]

    def fetch(tile):
        slot = tile % 2
        _start_row_gather(src_ref, tile * tm, x_hbm, xbuf.at[slot], sem.at[slot])

    @pl.when(i == 0)
    def _():
        fetch(i)

    @pl.when(i + 1 < n_used)
    def _():
        fetch(i + 1)

    @pl.when(i < n_used)
    def _():
        slot = i % 2
        _wait_row_gather(x_hbm, xbuf.at[slot], sem.at[slot])
        x = _unpack_halves(xbuf[slot]).astype(BF16)
        act = _silu(_dot(x, wg_ref[0])) * _dot(x, wu_ref[0])
        o_ref[...] = _pack_halves(_dot(act.astype(BF16), wd_ref[0]))

    @pl.when(i >= n_used)
    def _():
        o_ref[...] = jnp.zeros_like(o_ref)


def _route_plan(eidx, tm):
    t = eidx.shape[1]
    e = eidx.T.reshape(-1)
    one_hot = (e[:, None] == jnp.arange(N_EXPERTS, dtype=jnp.int32)[None, :]).astype(jnp.int32)
    csum = jnp.cumsum(one_hot, axis=0)
    rank = jnp.take_along_axis(csum, e[:, None], axis=1)[:, 0] - 1
    counts = csum[-1]
    padded = (counts + tm - 1) // tm * tm
    ends = jnp.cumsum(padded)
    pos = (ends - padded)[e] + rank
    n_slots = 2 * t + N_EXPERTS * tm
    n_tiles = n_slots // tm
    src = jnp.zeros((n_slots,), jnp.int32).at[pos].set(jnp.arange(2 * t, dtype=jnp.int32) // 2)
    n_used = (ends[-1] // tm).astype(jnp.int32)
    tile = jnp.arange(n_tiles, dtype=jnp.int32)
    owner = jnp.searchsorted(ends, jnp.minimum(tile, n_used - 1) * tm, side="right").astype(jnp.int32)
    pos_by_choice = pos.reshape(t, 2).T.reshape(-1)
    return src, pos_by_choice, jnp.minimum(owner, N_EXPERTS - 1), n_used.reshape(1)


def _moe_experts(x_packed, src, tile_expert, n_used, w_gate, w_up, w_down):
    n_slots = src.shape[0]
    dp = x_packed.shape[1]
    tm = MOE_TM
    d, de = w_gate.shape[1], w_gate.shape[2]

    def weights(i, te, nu, sr):
        return (te[i], 0, 0)

    return pl.pallas_call(
        _moe_kernel,
        out_shape=jax.ShapeDtypeStruct((n_slots, dp), jnp.uint32),
        grid_spec=pltpu.PrefetchScalarGridSpec(
            num_scalar_prefetch=3,
            grid=(n_slots // tm,),
            in_specs=[pl.BlockSpec(memory_space=pl.ANY),
                      pl.BlockSpec((1, d, de), weights),
                      pl.BlockSpec((1, d, de), weights),
                      pl.BlockSpec((1, de, d), weights)],
            out_specs=pl.BlockSpec((tm, dp), lambda i, te, nu, sr: (i, 0)),
            scratch_shapes=[pltpu.VMEM((2, tm, dp), jnp.uint32), pltpu.SemaphoreType.DMA((2,))]),
        compiler_params=_params("arbitrary"),
        name="moe_experts",
    )(tile_expert, n_used, src, x_packed, w_gate, w_up, w_down)


def _ln2_kernel(pos_ref, h_ref, y_hbm, w_ref, g_ref, b_ref, of_ref, ob_ref, rbuf, sem, *, alpha):
    i = pl.program_id(0)
    n_tiles = pl.num_programs(0)
    tm = h_ref.shape[0]

    def fetch(tile):
        slot = tile % 2
        for k in range(2):
            _start_row_gather(pos_ref, (k * n_tiles + tile) * tm, y_hbm, rbuf.at[slot, k], sem.at[slot, k])

    @pl.when(i == 0)
    def _():
        fetch(i)

    @pl.when(i + 1 < n_tiles)
    def _():
        fetch(i + 1)

    slot = i % 2
    y = alpha * h_ref[...]
    w = w_ref[...]
    for k in range(2):
        _wait_row_gather(y_hbm, rbuf.at[slot, k], sem.at[slot, k])
        y = y + w[:, k:k + 1] * _unpack_halves(rbuf[slot, k])
    out = _layer_norm(y, g_ref[...], b_ref[...])
    of_ref[...] = out
    ob_ref[...] = out.astype(BF16)


def _combine_ln(h, y_packed, pos_by_choice, wts, g, b, alpha):
    t, d = h.shape
    dp = y_packed.shape[1]
    tm = min(ROW_TM, t)
    row = pl.BlockSpec((tm, d), lambda i, pos: (i, 0))
    vec = pl.BlockSpec((1, d), lambda i, pos: (0, 0))
    return pl.pallas_call(
        functools.partial(_ln2_kernel, alpha=alpha),
        out_shape=(jax.ShapeDtypeStruct((t, d), F32), jax.ShapeDtypeStruct((t, d), BF16)),
        grid_spec=pltpu.PrefetchScalarGridSpec(
            num_scalar_prefetch=1,
            grid=(t // tm,),
            in_specs=[row, pl.BlockSpec(memory_space=pl.ANY), pl.BlockSpec((tm, 2), lambda i, pos: (i, 0)), vec, vec],
            out_specs=(row, row),
            scratch_shapes=[pltpu.VMEM((2, 2, tm, dp), jnp.uint32), pltpu.SemaphoreType.DMA((2, 2))]),
        compiler_params=_params("arbitrary"),
        name="combine_ln",
    )(pos_by_choice, h, y_packed, wts, g.reshape(1, d), b.reshape(1, d))


def _layer(h, h_b, bsz, seq, alpha, w_in, b_gate, conv_w, conv_b, dt_bias, a_log, d_skip, ssm_norm_g,
           w_up_ssd, w_up_sb, w_up_dil, w_out, ln1_g, ln1_b, router_w, router_bias,
           w_gate_e, w_up_e, w_down_e, ln2_g, ln2_b):
    t = bsz * seq
    dil_w = 3 * DIL_WIDTH
    w_main = jnp.concatenate([w_in[:, :_IN_DT0], w_in[:, _IN_DT1:_IN_DIL0 + dil_w], w_in[:, _IN_GATE0:]],
                             axis=1).astype(BF16)
    w_dt = w_in[:, _IN_DT0:_IN_DT1].reshape(D_MODEL, SSD_GROUPS, SSD_HEADS_PER_GROUP)
    w_dt = jnp.pad(w_dt, ((0, 0), (0, 0), (0, DT_LANES - SSD_HEADS_PER_GROUP))).reshape(D_MODEL, DT_WIDTH)
    proj2d = _matmul(h_b, w_main, BF16, MM_TM, MM_TN)
    dt_raw = _matmul(h_b, w_dt.astype(BF16), F32, MM_TM, DT_WIDTH)
    proj = proj2d.reshape(bsz, seq, N_PROJ)

    y_ssd = _ssd(proj, dt_raw.reshape(bsz, seq, DT_WIDTH), conv_w, conv_b, dt_bias, a_log, d_skip, ssm_norm_g)
    y_sb = _stick_breaking(proj)

    assert DIL_PAIRS[0][1] == 1 and all(w // d == DIL_BLOCK for w, d in DIL_PAIRS)
    o, lse = _band_attention(proj, COL_DIL)
    outs, lses = [o.reshape(t, DIL_WIDTH)], [lse.reshape(t, V7X_LANES)]
    for g, (_, dilation) in enumerate(DIL_PAIRS[1:], start=1):
        w_g = w_in[:, _IN_DIL0 + g * dil_w:_IN_DIL0 + (g + 1) * dil_w].astype(BF16)
        proj_g = _matmul(_by_residue(h_b, bsz, seq, dilation), w_g, BF16, MM_TM, MM_TN)
        o, lse = _band_attention(proj_g.reshape(bsz * dilation, seq // dilation, dil_w), 0)
        outs.append(_from_residue(o.reshape(t, DIL_WIDTH), bsz, seq, dilation))
        lses.append(_from_residue(lse.reshape(t, V7X_LANES), bsz, seq, dilation))
    y_dil = _dilated_mixture(outs, lses)

    merged = _merge_up(y_ssd.reshape(t, SSD_INNER), y_sb.reshape(t, SB_WIDTH), y_dil,
                       w_up_ssd.astype(BF16), w_up_sb.astype(BF16), w_up_dil.astype(BF16),
                       proj2d, b_gate.astype(F32).reshape(N_BRANCH, D_MODEL))
    h1, h1_p = _out_ln(merged, w_out.astype(BF16), h, ln1_g, ln1_b, alpha)

    eidx, wts = _router(h1, router_w, router_bias)
    src, pos_by_choice, tile_expert, n_used = _route_plan(eidx, MOE_TM)
    y_p = _moe_experts(h1_p, src, tile_expert, n_used,
                       w_gate_e.astype(BF16), w_up_e.astype(BF16), w_down_e.astype(BF16))
    return _combine_ln(h1, y_p, pos_by_choice, wts.T, ln2_g, ln2_b, alpha)


def kernel(x, w_in, b_gate, conv_w, conv_b, dt_bias, a_log, d_skip, ssm_norm_g, w_up_ssd, w_up_sb, w_up_dil,
           w_out, ln1_g, ln1_b, router_w, router_bias, w_gate_e, w_up_e, w_down_e, ln2_g, ln2_b):
    bsz, seq, d = x.shape
    depth = w_in.shape[0]
    alpha = (2 * depth) ** 0.25
    h = x.reshape(bsz * seq, d)
    h_b = h.astype(BF16)
    for l in range(depth):
        h, h_b = _layer(h, h_b, bsz, seq, alpha, w_in[l], b_gate[l], conv_w[l], conv_b[l], dt_bias[l], a_log[l],
                        d_skip[l], ssm_norm_g[l], w_up_ssd[l], w_up_sb[l], w_up_dil[l], w_out[l],
                        ln1_g[l], ln1_b[l], router_w, router_bias, w_gate_e[l], w_up_e[l], w_down_e[l],
                        ln2_g[l], ln2_b[l])
    return h.reshape(bsz, seq, d)
```

```python
import functools

import jax
import jax.numpy as jnp
from jax import lax
from jax.experimental import pallas as pl
from jax.experimental.pallas import tpu as pltpu

F32 = jnp.float32
BF16 = jnp.bfloat16

D_MODEL = 2048
SSD_HEADS = 32
SSD_HEAD_DIM = 64
SSD_INNER = SSD_HEADS * SSD_HEAD_DIM
SSD_STATE = 128
SSD_GROUPS = 4
SSD_HEADS_PER_GROUP = SSD_HEADS // SSD_GROUPS
SSD_GROUP_WIDTH = SSD_HEADS_PER_GROUP * SSD_HEAD_DIM
SSD_CONV = 4
SSD_CHUNK = 256
SSD_BC = SSD_GROUPS * SSD_STATE
SB_HEADS = 16
SB_HEAD_DIM = 128
SB_WIDTH = SB_HEADS * SB_HEAD_DIM
DIL_PAIRS = ((128, 1), (512, 4), (2048, 16))
DIL_HEADS = 8
DIL_HEAD_DIM = 128
DIL_WIDTH = DIL_HEADS * DIL_HEAD_DIM
DIL_BLOCK = 128
N_BRANCH = 3
N_EXPERTS = 16
EXPERTS_PER_GROUP = 4
N_EXPERT_GROUPS = N_EXPERTS // EXPERTS_PER_GROUP
D_EXPERT = 1024
EPS = 1e-5
NEG = -1e30
LOG2E = 1.4426950408889634
F32_SUBNORMAL_EXP2 = -128.0

_IN_DT0 = SSD_INNER + SSD_INNER + 2 * SSD_BC
_IN_DT1 = _IN_DT0 + SSD_HEADS
_IN_DIL0 = _IN_DT1 + 3 * SB_WIDTH
_IN_GATE0 = _IN_DIL0 + 3 * len(DIL_PAIRS) * DIL_WIDTH
COL_Z = 0
COL_X = SSD_INNER
COL_B = COL_X + SSD_INNER
COL_C = COL_B + SSD_BC
COL_SB = COL_C + SSD_BC
COL_DIL = COL_SB + 3 * SB_WIDTH
COL_GATE = COL_DIL + 3 * DIL_WIDTH
N_PROJ = COL_GATE + N_BRANCH * D_MODEL
DT_LANES = 128
DT_WIDTH = SSD_GROUPS * DT_LANES

V7X_LANES = 128
V7X_VMEM_LIMIT = 48 * 1024 * 1024
MM_TM, MM_TN = 1024, 1024
MERGE_TM, MERGE_TN = 512, 512
ROW_TM = 256
SB_BLOCK = 256
SB_HEADS_PER_STEP = 4
SB_EXP2_CLAMP = 64.0
MOE_TM = 256


def _params(*sem):
    return pltpu.CompilerParams(dimension_semantics=sem, vmem_limit_bytes=V7X_VMEM_LIMIT)


def _dot(a, b):
    return jnp.dot(a, b, preferred_element_type=F32)


def _dot_nt(a, b):
    return lax.dot_general(a, b, (((1,), (1,)), ((), ())), preferred_element_type=F32)


def _split_dot(x, m, passes):
    acc = None
    r = x
    for p in range(passes):
        s = r.astype(BF16)
        d = _dot(s, m)
        acc = d if acc is None else acc + d
        if p + 1 < passes:
            r = r - s.astype(F32)
    return acc


def _split_dot_left(m, x, passes):
    acc = None
    r = x
    for p in range(passes):
        s = r.astype(BF16)
        d = _dot(m, s)
        acc = d if acc is None else acc + d
        if p + 1 < passes:
            r = r - s.astype(F32)
    return acc


def _softplus(x):
    return jnp.maximum(x, 0.0) + jnp.log1p(jnp.exp(-jnp.abs(x)))


def _silu(x):
    return x * jax.nn.sigmoid(x)


def _layer_norm(y, g, b):
    mu = jnp.mean(y, axis=-1, keepdims=True)
    yc = y - mu
    var = jnp.mean(yc * yc, axis=-1, keepdims=True)
    return yc * lax.rsqrt(var + EPS) * g + b


def _pack_halves(x):
    n = x.shape[1] // 2
    bits = pltpu.bitcast(x.astype(BF16).astype(F32), jnp.uint32)
    return (bits[:, n:] & jnp.uint32(0xFFFF0000)) | (bits[:, :n] >> 16)


def _unpack_halves(p):
    lo = pltpu.bitcast(p << 16, F32)
    hi = pltpu.bitcast(p & jnp.uint32(0xFFFF0000), F32)
    return jnp.concatenate([lo, hi], axis=1)


def _start_row_gather(idx_ref, base, src_hbm, dst, sem):
    def issue(r, carry):
        pltpu.make_async_copy(src_hbm.at[pl.ds(idx_ref[base + r], 1)], dst.at[pl.ds(r, 1)], sem).start()
        return carry

    lax.fori_loop(0, dst.shape[0], issue, 0, unroll=8)


def _wait_row_gather(src_hbm, dst, sem):
    pltpu.make_async_copy(src_hbm.at[pl.ds(0, dst.shape[0])], dst, sem).wait()


def _expander(rows, cols, width):
    h = lax.broadcasted_iota(jnp.int32, (rows, cols), 0)
    j = lax.broadcasted_iota(jnp.int32, (rows, cols), 1)
    return jnp.where(h == j // width, 1.0, 0.0).astype(BF16)


def _mm_kernel(a_ref, w_ref, o_ref):
    o_ref[...] = _dot(a_ref[...], w_ref[...]).astype(o_ref.dtype)


def _matmul(a, w, out_dtype, tm, tn):
    m, k = a.shape
    n = w.shape[1]
    tm, tn = min(tm, m), min(tn, n)
    return pl.pallas_call(
        _mm_kernel,
        out_shape=jax.ShapeDtypeStruct((m, n), out_dtype),
        grid=(m // tm, n // tn),
        in_specs=[pl.BlockSpec((tm, k), lambda i, j: (i, 0)),
                  pl.BlockSpec((k, tn), lambda i, j: (0, j))],
        out_specs=pl.BlockSpec((tm, tn), lambda i, j: (i, j)),
        compiler_params=_params("parallel", "parallel"),
        name="matmul",
    )(a, w)


def _ssd_kernel(z_ref, x_ref, b_ref, c_ref, dt_ref, cwx_ref, cwb_ref, cwc_ref, cbx_ref, cbb_ref, cbc_ref,
                dtb_ref, alog_ref, dskip_ref, g_ref, o_ref,
                state_ref, xpad_ref, bpad_ref, cpad_ref, xs_ref, bs_ref, cs_ref, y_ref,
                dts_ref, acs_ref, ecs_ref, dte_ref):
    chunk = x_ref.shape[1]
    pad = 8

    @pl.when(pl.program_id(1) == 0)
    def _():
        state_ref[...] = jnp.zeros_like(state_ref)
        xpad_ref[0:pad, :] = jnp.zeros((pad, xpad_ref.shape[1]), F32)
        bpad_ref[0:pad, :] = jnp.zeros((pad, bpad_ref.shape[1]), F32)
        cpad_ref[0:pad, :] = jnp.zeros((pad, cpad_ref.shape[1]), F32)

    def conv_silu(src_ref, pad_ref, cw_ref, cb_ref, dst_ref):
        pad_ref[pad:pad + chunk, :] = src_ref[0].astype(F32)
        acc = cb_ref[...]
        for k in range(SSD_CONV):
            off = pad - (SSD_CONV - 1) + k
            acc = acc + cw_ref[k:k + 1, :] * pad_ref[off:off + chunk, :]
        dst_ref[...] = _silu(acc)
        pad_ref[0:pad, :] = pad_ref[chunk:chunk + pad, :]

    conv_silu(x_ref, xpad_ref, cwx_ref, cbx_ref, xs_ref)
    conv_silu(b_ref, bpad_ref, cwb_ref, cbb_ref, bs_ref)
    conv_silu(c_ref, cpad_ref, cwc_ref, cbc_ref, cs_ref)

    dt = _softplus(dt_ref[0] + dtb_ref[...])
    a_dt = dt * (-jnp.exp(alog_ref[...]))
    row = lax.broadcasted_iota(jnp.int32, (chunk, chunk), 0)
    col = lax.broadcasted_iota(jnp.int32, (chunk, chunk), 1)
    causal = col <= row
    tri = jnp.where(causal, 1.0, 0.0).astype(BF16)
    a_cs = _split_dot_left(tri, a_dt, 3)
    last = a_cs[chunk - 1:chunk, :]
    dts_ref[...] = dt
    acs_ref[...] = a_cs
    ecs_ref[...] = jnp.exp(a_cs)
    dte_ref[...] = jnp.exp(last - a_cs)

    expand = _expander(DT_LANES, SSD_GROUP_WIDTH, SSD_HEAD_DIM)
    lane = lax.broadcasted_iota(jnp.int32, (chunk, V7X_LANES), 1)
    heads_per_tile = V7X_LANES // SSD_HEAD_DIM

    def group(g, carry):
        ch0 = pl.multiple_of(g * SSD_GROUP_WIDTH, SSD_GROUP_WIDTH)
        st0 = pl.multiple_of(g * SSD_STATE, SSD_STATE)
        dt0 = pl.multiple_of(g * DT_LANES, DT_LANES)
        xg = xs_ref[:, pl.ds(ch0, SSD_GROUP_WIDTH)]
        bg = bs_ref[:, pl.ds(st0, SSD_STATE)]
        cg = cs_ref[:, pl.ds(st0, SSD_STATE)].astype(BF16)
        csg = acs_ref[:, pl.ds(dt0, DT_LANES)]
        dt_x = _split_dot(dts_ref[:, pl.ds(dt0, DT_LANES)], expand, 2)
        ecs_x = _split_dot(ecs_ref[:, pl.ds(dt0, DT_LANES)], expand, 2)
        dte_x = _split_dot(dte_ref[:, pl.ds(dt0, DT_LANES)], expand, 2)
        xdt = xg * dt_x
        xdt_b = xdt.astype(BF16)
        cb = _dot_nt(cg, bg.astype(BF16))
        cs_t = csg.T
        prev = state_ref[g]
        y_off = _dot(cg, prev.astype(BF16)) * ecs_x
        for t in range(SSD_GROUP_WIDTH // V7X_LANES):
            lanes = slice(t * V7X_LANES, (t + 1) * V7X_LANES)
            xdt_t = xdt_b[:, lanes]
            y_t = None
            for u in range(heads_per_tile):
                r = t * heads_per_tile + u
                seg = csg[:, r:r + 1] - cs_t[r:r + 1, :]
                m = (cb * jnp.exp(jnp.where(causal, seg, NEG))).astype(BF16)
                y_r = _dot(m, xdt_t)
                y_t = y_r if y_t is None else jnp.where(lane // SSD_HEAD_DIM == u, y_r, y_t)
            c0 = pl.multiple_of(ch0 + t * V7X_LANES, V7X_LANES)
            y_ref[:, pl.ds(c0, V7X_LANES)] = (y_t + y_off[:, lanes]
                                              + dskip_ref[:, pl.ds(c0, V7X_LANES)] * xg[:, lanes])
        new = ecs_x[chunk - 1:chunk, :] * prev + _dot(bg.T.astype(BF16), (xdt * dte_x).astype(BF16))
        state_ref[g] = new
        return carry

    lax.fori_loop(0, SSD_GROUPS, group, 0)

    yf = y_ref[...] * _silu(z_ref[0].astype(F32))
    ms = jnp.mean(yf * yf, axis=-1, keepdims=True)
    o_ref[0] = (yf * lax.rsqrt(ms + EPS) * g_ref[...]).astype(o_ref.dtype)


def _ssd(proj, dt_raw, conv_w, conv_b, dt_bias, a_log, d_skip, norm_g):
    bsz, seq, _ = proj.shape
    chunk = min(SSD_CHUNK, seq)
    nc = seq // chunk

    def spread_heads(v):
        v = v.astype(F32).reshape(SSD_GROUPS, SSD_HEADS_PER_GROUP)
        return jnp.pad(v, ((0, 0), (0, DT_LANES - SSD_HEADS_PER_GROUP))).reshape(1, DT_WIDTH)

    cw = conv_w.astype(F32)
    cbias = conv_b.astype(F32).reshape(1, -1)
    x1, b1 = SSD_INNER, SSD_INNER + SSD_BC
    consts = [cw[:, :x1], cw[:, x1:b1], cw[:, b1:], cbias[:, :x1], cbias[:, x1:b1], cbias[:, b1:],
              spread_heads(dt_bias), spread_heads(a_log),
              jnp.repeat(d_skip.astype(F32), SSD_HEAD_DIM).reshape(1, SSD_INNER),
              norm_g.astype(F32).reshape(1, SSD_INNER)]

    def col_block(width, col):
        return pl.BlockSpec((1, chunk, width), lambda b, c: (b, c, col // width))

    def whole(a):
        return pl.BlockSpec(a.shape, lambda b, c: (0, 0))

    return pl.pallas_call(
        _ssd_kernel,
        out_shape=jax.ShapeDtypeStruct((bsz, seq, SSD_INNER), BF16),
        grid=(bsz, nc),
        in_specs=[col_block(SSD_INNER, COL_Z), col_block(SSD_INNER, COL_X),
                  col_block(SSD_BC, COL_B), col_block(SSD_BC, COL_C),
                  pl.BlockSpec((1, chunk, DT_WIDTH), lambda b, c: (b, c, 0))] + [whole(a) for a in consts],
        out_specs=pl.BlockSpec((1, chunk, SSD_INNER), lambda b, c: (b, c, 0)),
        scratch_shapes=[pltpu.VMEM((SSD_GROUPS, SSD_STATE, SSD_GROUP_WIDTH), F32),
                        pltpu.VMEM((chunk + 8, SSD_INNER), F32),
                        pltpu.VMEM((chunk + 8, SSD_BC), F32),
                        pltpu.VMEM((chunk + 8, SSD_BC), F32),
                        pltpu.VMEM((chunk, SSD_INNER), F32),
                        pltpu.VMEM((chunk, SSD_BC), F32),
                        pltpu.VMEM((chunk, SSD_BC), F32),
                        pltpu.VMEM((chunk, SSD_INNER), F32),
                        pltpu.VMEM((chunk, DT_WIDTH), F32),
                        pltpu.VMEM((chunk, DT_WIDTH), F32),
                        pltpu.VMEM((chunk, DT_WIDTH), F32),
                        pltpu.VMEM((chunk, DT_WIDTH), F32)],
        compiler_params=_params("parallel", "arbitrary"),
        name="ssd",
    )(proj, proj, proj, proj, dt_raw, *consts)


def _sb_kernel(q_ref, k_ref, v_ref, o_ref, *, blk, scale2):
    qi = pl.program_id(2)
    hd = SB_HEAD_DIM
    row = lax.broadcasted_iota(jnp.int32, (blk, blk), 0)
    col = lax.broadcasted_iota(jnp.int32, (blk, blk), 1)
    strict = col < row
    upper = jnp.where(row >= col, 1.0, 0.0).astype(BF16)
    upper2 = jnp.concatenate([upper, upper], axis=0)

    lanes = [slice(h * hd, (h + 1) * hd) for h in range(q_ref.shape[2] // hd)]

    def blocks(kb, rs, diagonal):
        start = pl.multiple_of(kb * blk, blk)
        zs = [_dot_nt(q_ref[0, :, sl], k_ref[0, pl.ds(start, blk), sl]) * scale2 for sl in lanes]
        sufs = []
        for z in zs:
            sp = jnp.maximum(z, jnp.log2(1.0 + jnp.exp2(jnp.minimum(z, SB_EXP2_CLAMP))))
            if diagonal:
                sp = jnp.where(strict, sp, 0.0)
            hi = pltpu.bitcast(pltpu.bitcast(sp, jnp.uint32) & jnp.uint32(0xFFFF0000), F32)
            parts = jnp.concatenate([hi.astype(BF16), (sp - hi).astype(BF16)], axis=1)
            sufs.append(_dot(parts, upper2))
        new_rs = tuple(r - suf[:, 0:1] for r, suf in zip(rs, sufs))
        pvs = []
        for z, suf, r, sl in zip(zs, sufs, rs, lanes):
            w = jnp.exp2(z - suf + r)
            if diagonal:
                w = jnp.where(strict, w, 0.0)
            pvs.append(_dot(w.astype(BF16), v_ref[0, pl.ds(start, blk), sl]))
        return new_rs, pvs

    def largest(rs):
        m = rs[0]
        for r in rs[1:]:
            m = jnp.maximum(m, r)
        return jnp.max(m)

    rs, accs = blocks(qi, tuple(jnp.zeros((blk, 1), F32) for _ in lanes), True)

    def more(c):
        return (c[0] >= 0) & (c[1] > F32_SUBNORMAL_EXP2)

    def body(c):
        kb, _, rs, accs = c
        rs, pvs = blocks(kb, rs, False)
        return kb - 1, largest(rs), rs, tuple(a + pv for a, pv in zip(accs, pvs))

    _, _, _, accs = lax.while_loop(more, body, (qi - 1, largest(rs), rs, tuple(accs)))
    o_ref[0] = jnp.concatenate(accs, axis=1).astype(o_ref.dtype)


def _stick_breaking(proj):
    bsz, seq, _ = proj.shape
    blk = min(SB_BLOCK, seq)
    qc, kc, vc = (COL_SB + i * SB_WIDTH for i in range(3))
    w = SB_HEADS_PER_STEP * SB_HEAD_DIM
    return pl.pallas_call(
        functools.partial(_sb_kernel, blk=blk, scale2=SB_HEAD_DIM ** -0.5 * LOG2E),
        out_shape=jax.ShapeDtypeStruct((bsz, seq, SB_WIDTH), BF16),
        grid=(bsz, SB_WIDTH // w, seq // blk),
        in_specs=[pl.BlockSpec((1, blk, w), lambda b, h, i: (b, i, qc // w + h)),
                  pl.BlockSpec((1, seq, w), lambda b, h, i: (b, 0, kc // w + h)),
                  pl.BlockSpec((1, seq, w), lambda b, h, i: (b, 0, vc // w + h))],
        out_specs=pl.BlockSpec((1, blk, w), lambda b, h, i: (b, i, h)),
        compiler_params=_params("parallel", "parallel", "arbitrary"),
        name="stick_breaking",
    )(proj, proj, proj)


def _dil_kernel(q_ref, kp_ref, kc_ref, vp_ref, vc_ref, o_ref, lse_ref, *, scale):
    bq = DIL_BLOCK
    blk = pl.program_id(1)
    i = lax.broadcasted_iota(jnp.int32, (bq, 2 * bq), 0)
    m = lax.broadcasted_iota(jnp.int32, (bq, 2 * bq), 1)
    dist = i - m + bq
    first = jnp.where(blk > 0, 0, bq)
    valid = (dist >= 0) & (dist <= bq) & (m >= first)
    lane = lax.broadcasted_iota(jnp.int32, (bq, V7X_LANES), 1)
    lse_all = jnp.zeros((bq, V7X_LANES), F32)
    for h in range(DIL_HEADS):
        sl = slice(h * DIL_HEAD_DIM, (h + 1) * DIL_HEAD_DIM)
        k = jnp.concatenate([kp_ref[0, :, sl], kc_ref[0, :, sl]], axis=0)
        v = jnp.concatenate([vp_ref[0, :, sl], vc_ref[0, :, sl]], axis=0)
        s = jnp.where(valid, _dot_nt(q_ref[0, :, sl], k) * scale, NEG)
        mx = jnp.max(s, axis=-1, keepdims=True)
        p = jnp.exp(s - mx)
        den = jnp.sum(p, axis=-1, keepdims=True)
        o_ref[0, :, sl] = (_dot(p.astype(BF16), v) / den).astype(o_ref.dtype)
        lse_all = jnp.where(lane == h, mx + jnp.log(den), lse_all)
    lse_ref[0] = lse_all


def _band_attention(qkv, col):
    nz, lc, _ = qkv.shape
    bq = DIL_BLOCK
    w = DIL_WIDTH
    assert lc % bq == 0 and col % w == 0

    def cur(part):
        return pl.BlockSpec((1, bq, w), lambda z, i: (z, i, col // w + part))

    def prev(part):
        return pl.BlockSpec((1, bq, w), lambda z, i: (z, jnp.maximum(i - 1, 0), col // w + part))

    return pl.pallas_call(
        functools.partial(_dil_kernel, scale=DIL_HEAD_DIM ** -0.5),
        out_shape=(jax.ShapeDtypeStruct((nz, lc, w), BF16),
                   jax.ShapeDtypeStruct((nz, lc, V7X_LANES), F32)),
        grid=(nz, lc // bq),
        in_specs=[cur(0), prev(1), cur(1), prev(2), cur(2)],
        out_specs=(pl.BlockSpec((1, bq, w), lambda z, i: (z, i, 0)),
                   pl.BlockSpec((1, bq, V7X_LANES), lambda z, i: (z, i, 0))),
        compiler_params=_params("parallel", "arbitrary"),
        name="dilated_attention",
    )(qkv, qkv, qkv, qkv, qkv)


def _residue_block(tm, seq, dilation, width):
    tiles = seq // tm
    return pl.BlockSpec((1, dilation, tm // dilation, width), lambda i, *_: (i // tiles, 0, i % tiles, 0))


def _emit_by_residue(x, strip_ref, dst_refs):
    tm, w = x.shape
    strips = w // V7X_LANES
    for s in range(strips):
        strip_ref[s] = x[:, s * V7X_LANES:(s + 1) * V7X_LANES]
    for dst in dst_refs:
        d, n = dst.shape[1], dst.shape[2]
        for c in range(d):
            for s in range(strips):
                dst[0, c, :, s * V7X_LANES:(s + 1) * V7X_LANES] = strip_ref[s, pl.ds(c, n, stride=d), :].astype(dst.dtype)


def _dilmix_kernel(o0_ref, l0_ref, o1_ref, l1_ref, o2_ref, l2_ref, out_ref, strip_ref, lse_ref):
    tm = out_ref.shape[0]
    heads = DIL_WIDTH // DIL_HEAD_DIM
    for gi, (o_ref, l_ref) in enumerate(((o1_ref, l1_ref), (o2_ref, l2_ref))):
        d = o_ref.shape[1]
        n = tm // d
        for c in range(d):
            lse_ref[gi, pl.ds(c, n, stride=d), :] = l_ref[0, c]
            for j in range(heads):
                strip_ref[gi, j, pl.ds(c, n, stride=d), :] = (
                    o_ref[0, c, :, j * DIL_HEAD_DIM:(j + 1) * DIL_HEAD_DIM].astype(F32))
    ls = [l0_ref[...], lse_ref[0], lse_ref[1]]
    mx = jnp.maximum(jnp.maximum(ls[0], ls[1]), ls[2])
    es = [jnp.exp(l - mx) for l in ls]
    inv = 1.0 / (es[0] + es[1] + es[2])
    ws = [e * inv for e in es]
    for j in range(heads):
        sl = slice(j * DIL_HEAD_DIM, (j + 1) * DIL_HEAD_DIM)
        acc = (ws[0][:, j:j + 1] * o0_ref[:, sl].astype(F32) + ws[1][:, j:j + 1] * strip_ref[0, j]
               + ws[2][:, j:j + 1] * strip_ref[1, j])
        out_ref[:, sl] = acc.astype(out_ref.dtype)


def _dilated_mixture(o0, l0, o1, l1, o2, l2, seq):
    t = o0.shape[0]
    tm = min(ROW_TM, t)
    heads = DIL_WIDTH // DIL_HEAD_DIM
    assert DIL_HEAD_DIM == V7X_LANES
    o_spec = pl.BlockSpec((tm, DIL_WIDTH), lambda i: (i, 0))
    l_spec = pl.BlockSpec((tm, V7X_LANES), lambda i: (i, 0))
    res = [spec for o in (o1, o2) for spec in (_residue_block(tm, seq, o.shape[1], DIL_WIDTH),
                                               _residue_block(tm, seq, o.shape[1], V7X_LANES))]
    return pl.pallas_call(
        _dilmix_kernel,
        out_shape=jax.ShapeDtypeStruct((t, DIL_WIDTH), BF16),
        grid=(t // tm,),
        in_specs=[o_spec, l_spec] + res,
        out_specs=o_spec,
        scratch_shapes=[pltpu.VMEM((2, heads, tm, DIL_HEAD_DIM), F32), pltpu.VMEM((2, tm, V7X_LANES), F32)],
        compiler_params=_params("parallel"),
        name="dilated_mix",
    )(o0, l0, o1, l1, o2, l2)


def _merge_kernel(ya_ref, yb_ref, yc_ref, ua_ref, ub_ref, uc_ref, ga_ref, gb_ref, gc_ref, bg_ref, o_ref):
    acc = None
    for i, (y_ref, u_ref, g_ref) in enumerate(((ya_ref, ua_ref, ga_ref), (yb_ref, ub_ref, gb_ref),
                                               (yc_ref, uc_ref, gc_ref))):
        gate = jax.nn.sigmoid(g_ref[...].astype(F32) + bg_ref[i:i + 1, :])
        term = gate * _dot(y_ref[...], u_ref[...])
        acc = term if acc is None else acc + term
    o_ref[...] = acc.astype(o_ref.dtype)


def _merge_up(y_ssd, y_sb, y_dil, u_ssd, u_sb, u_dil, proj2d, b_gate):
    t = y_ssd.shape[0]
    tm, tn = min(MERGE_TM, t), MERGE_TN

    def rows(a):
        return pl.BlockSpec((tm, a.shape[1]), lambda i, j: (i, 0))

    def cols(a):
        return pl.BlockSpec((a.shape[0], tn), lambda i, j: (0, j))

    def gate(k):
        return pl.BlockSpec((tm, tn), lambda i, j: (i, (COL_GATE + k * D_MODEL) // tn + j))

    return pl.pallas_call(
        _merge_kernel,
        out_shape=jax.ShapeDtypeStruct((t, D_MODEL), BF16),
        grid=(t // tm, D_MODEL // tn),
        in_specs=[rows(y_ssd), rows(y_sb), rows(y_dil), cols(u_ssd), cols(u_sb), cols(u_dil),
                  gate(0), gate(1), gate(2), pl.BlockSpec((N_BRANCH, tn), lambda i, j: (0, j))],
        out_specs=pl.BlockSpec((tm, tn), lambda i, j: (i, j)),
        compiler_params=_params("parallel", "parallel"),
        name="merge_up",
    )(y_ssd, y_sb, y_dil, u_ssd, u_sb, u_dil, proj2d, proj2d, proj2d, b_gate)


def _outln_kernel(m_ref, w_ref, h_ref, g_ref, b_ref, rwh_ref, rwl_ref, rb_ref, of_ref, op_ref, meta_ref, wts_ref,
                  cnt_ref, *, alpha):
    @pl.when(pl.program_id(0) == 0)
    def _():
        cnt_ref[...] = jnp.zeros_like(cnt_ref)

    y = alpha * h_ref[...] + _dot(m_ref[...], w_ref[...])
    out = _layer_norm(y, g_ref[...], b_ref[...])
    of_ref[...] = out
    op_ref[...] = _pack_halves(out)
    e1, e2, w1, w2 = _route(out, rwh_ref[...], rwl_ref[...], rb_ref[...])
    r1, r2 = _expert_ranks(e1, e2, cnt_ref)
    for row, v in enumerate((e1, e2, r1, r2)):
        meta_ref[row:row + 1, :] = v
    wts_ref[0:1, :] = w1
    wts_ref[1:2, :] = w2


def _out_ln(merged, w_out, h, g, b, router_w, router_bias, alpha):
    t, d = h.shape
    tm = min(ROW_TM, t)
    row = pl.BlockSpec((tm, d), lambda i: (i, 0))
    vec = pl.BlockSpec((1, d), lambda i: (0, 0))
    rw = jnp.pad(router_w.astype(F32), ((0, 0), (0, V7X_LANES - N_EXPERTS)))
    rw_hi = rw.astype(BF16)
    rw_lo = (rw - rw_hi.astype(F32)).astype(BF16)
    rw_spec = pl.BlockSpec((d, V7X_LANES), lambda i: (0, 0))
    return pl.pallas_call(
        functools.partial(_outln_kernel, alpha=alpha),
        out_shape=(jax.ShapeDtypeStruct((t, d), F32), jax.ShapeDtypeStruct((t, d // 2), jnp.uint32),
                   jax.ShapeDtypeStruct((4, t), jnp.int32), jax.ShapeDtypeStruct((2, t), F32),
                   jax.ShapeDtypeStruct((N_EXPERTS, V7X_LANES), F32)),
        grid=(t // tm,),
        in_specs=[row, pl.BlockSpec((d, d), lambda i: (0, 0)), row, vec, vec,
                  rw_spec, rw_spec, pl.BlockSpec((N_EXPERTS, 1), lambda i: (0, 0))],
        out_specs=(row, pl.BlockSpec((tm, d // 2), lambda i: (i, 0)),
                   pl.BlockSpec((4, tm), lambda i: (0, i)), pl.BlockSpec((2, tm), lambda i: (0, i)),
                   pl.BlockSpec((N_EXPERTS, V7X_LANES), lambda i: (0, 0))),
        compiler_params=_params("arbitrary"),
        name="out_ln",
    )(merged, w_out, h, g.reshape(1, d), b.reshape(1, d),
      rw_hi, rw_lo, router_bias.astype(F32).reshape(N_EXPERTS, 1))


def _route(h, rw_hi, rw_lo, rb):
    h_hi = h.astype(BF16)
    h_lo = (h - h_hi.astype(F32)).astype(BF16)
    logits = _dot(h_hi, rw_hi) + (_dot(h_lo, rw_hi) + _dot(h_hi, rw_lo))
    logits = logits.T[0:N_EXPERTS, :]
    mx = jnp.max(logits, axis=0, keepdims=True)
    ex = jnp.exp(logits - mx)
    aff = ex / jnp.sum(ex, axis=0, keepdims=True)
    sel = aff + rb
    s = [sel[e:e + 1, :] for e in range(N_EXPERTS)]
    a = [aff[e:e + 1, :] for e in range(N_EXPERTS)]
    n = EXPERTS_PER_GROUP
    best_score, best = None, None
    for g in range(N_EXPERT_GROUPS):
        grp = s[g * n:(g + 1) * n]
        score = None
        for i in range(n):
            for j in range(i + 1, n):
                pair = grp[i] + grp[j]
                score = pair if score is None else jnp.maximum(score, pair)
        if best is None:
            best_score, best = score, jnp.zeros_like(score, dtype=jnp.int32)
        else:
            take = score > best_score
            best_score = jnp.where(take, score, best_score)
            best = jnp.where(take, g, best)
    sv, av = [], []
    for r in range(n):
        sr, ar = s[r], a[r]
        for g in range(1, N_EXPERT_GROUPS):
            sr = jnp.where(best == g, s[g * n + r], sr)
            ar = jnp.where(best == g, a[g * n + r], ar)
        sv.append(sr)
        av.append(ar)

    def arg_top(vals, skip):
        top_v, top_i, top_a = None, None, None
        for r in range(n):
            v = vals[r] if skip is None else jnp.where(skip == r, -jnp.inf, vals[r])
            if top_v is None:
                top_v, top_i, top_a = v, jnp.zeros_like(best), av[0]
            else:
                take = v > top_v
                top_v = jnp.where(take, v, top_v)
                top_i = jnp.where(take, r, top_i)
                top_a = jnp.where(take, av[r], top_a)
        return top_i, top_a

    i1, a1 = arg_top(sv, None)
    i2, a2 = arg_top(sv, i1)
    tot = a1 + a2
    return best * n + i1, best * n + i2, a1 / tot, a2 / tot


def _expert_ranks(e1, e2, count_ref):
    tm = e1.shape[1]
    expert = lax.broadcasted_iota(jnp.int32, (N_EXPERTS, tm), 0)
    hit1, hit2 = expert == e1, expert == e2
    both = jnp.where(hit1, 1.0, 0.0) + jnp.where(hit2, 1.0, 0.0)
    earlier = lax.broadcasted_iota(jnp.int32, (tm, tm), 0) < lax.broadcasted_iota(jnp.int32, (tm, tm), 1)
    before = _dot(both.astype(BF16), jnp.where(earlier, 1.0, 0.0).astype(BF16)) + count_ref[:, 0:1]
    r1 = jnp.sum(jnp.where(hit1, before, 0.0), axis=0, keepdims=True)
    r2 = jnp.sum(jnp.where(hit2, before, 0.0), axis=0, keepdims=True)
    count_ref[...] = count_ref[...] + jnp.sum(both, axis=1, keepdims=True)
    return r1.astype(jnp.int32), r2.astype(jnp.int32)


def _moe_kernel(te_ref, nu_ref, src_ref, x_hbm, wg_ref, wu_ref, wd_ref, o_ref, xbuf, sem):
    i = pl.program_id(0)
    n_used = nu_ref[0]
    tm = xbuf.shape[1]

    def fetch(tile):
        slot = tile % 2
        _start_row_gather(src_ref, tile * tm, x_hbm, xbuf.at[slot], sem.at[slot])

    @pl.when(i == 0)
    def _():
        fetch(i)

    @pl.when(i + 1 < n_used)
    def _():
        fetch(i + 1)

    @pl.when(i < n_used)
    def _():
        slot = i % 2
        _wait_row_gather(x_hbm, xbuf.at[slot], sem.at[slot])
        x = _unpack_halves(xbuf[slot]).astype(BF16)
        act = _silu(_dot(x, wg_ref[0])) * _dot(x, wu_ref[0])
        o_ref[...] = _pack_halves(_dot(act.astype(BF16), wd_ref[0]))

    @pl.when(i >= n_used)
    def _():
        o_ref[...] = jnp.zeros_like(o_ref)


def _route_plan(meta, counts, tm):
    t = meta.shape[1]
    padded = (counts + tm - 1) // tm * tm
    ends = jnp.cumsum(padded)
    starts = ends - padded
    one_hot = meta[0:2, :, None] == jnp.arange(N_EXPERTS, dtype=jnp.int32)
    pos = jnp.sum(jnp.where(one_hot, starts, 0), axis=-1) + meta[2:4]
    pos_by_choice = pos.reshape(-1)
    n_slots = 2 * t + N_EXPERTS * tm
    n_tiles = n_slots // tm
    token = jnp.tile(jnp.arange(t, dtype=jnp.int32), 2)
    src = jnp.zeros((n_slots,), jnp.int32).at[pos_by_choice].set(token)
    n_used = (ends[-1] // tm).astype(jnp.int32)
    tile = jnp.arange(n_tiles, dtype=jnp.int32)
    owner = jnp.searchsorted(ends, jnp.minimum(tile, n_used - 1) * tm, side="right").astype(jnp.int32)
    return src, pos_by_choice, jnp.minimum(owner, N_EXPERTS - 1), n_used.reshape(1)


def _moe_experts(x_packed, src, tile_expert, n_used, w_gate, w_up, w_down):
    n_slots = src.shape[0]
    dp = x_packed.shape[1]
    tm = MOE_TM
    d, de = w_gate.shape[1], w_gate.shape[2]

    def weights(i, te, nu, sr):
        return (te[i], 0, 0)

    return pl.pallas_call(
        _moe_kernel,
        out_shape=jax.ShapeDtypeStruct((n_slots, dp), jnp.uint32),
        grid_spec=pltpu.PrefetchScalarGridSpec(
            num_scalar_prefetch=3,
            grid=(n_slots // tm,),
            in_specs=[pl.BlockSpec(memory_space=pl.ANY),
                      pl.BlockSpec((1, d, de), weights),
                      pl.BlockSpec((1, d, de), weights),
                      pl.BlockSpec((1, de, d), weights)],
            out_specs=pl.BlockSpec((tm, dp), lambda i, te, nu, sr: (i, 0)),
            scratch_shapes=[pltpu.VMEM((2, tm, dp), jnp.uint32), pltpu.SemaphoreType.DMA((2,))]),
        compiler_params=_params("arbitrary"),
        name="moe_experts",
    )(tile_expert, n_used, src, x_packed, w_gate, w_up, w_down)


def _ln2_kernel(pos_ref, h_ref, y_hbm, w_ref, g_ref, b_ref, of_ref, *rest, alpha, n_views):
    views, (rbuf, sem), strips = rest[:n_views], rest[n_views:n_views + 2], rest[n_views + 2:]
    i = pl.program_id(0)
    n_tiles = pl.num_programs(0)
    tm = h_ref.shape[0]

    def fetch(tile):
        slot = tile % 2
        for k in range(2):
            _start_row_gather(pos_ref, (k * n_tiles + tile) * tm, y_hbm, rbuf.at[slot, k], sem.at[slot, k])

    @pl.when(i == 0)
    def _():
        fetch(i)

    @pl.when(i + 1 < n_tiles)
    def _():
        fetch(i + 1)

    slot = i % 2
    y = alpha * h_ref[...]
    w = w_ref[...]
    for k in range(2):
        _wait_row_gather(y_hbm, rbuf.at[slot, k], sem.at[slot, k])
        y = y + w[:, k:k + 1] * _unpack_halves(rbuf[slot, k])
    out = _layer_norm(y, g_ref[...], b_ref[...])
    of_ref[...] = out
    if views:
        views[0][...] = out.astype(BF16)
        _emit_by_residue(out, strips[0], views[1:])


def _view_shapes_specs(bsz, seq, d, tm):
    shapes = [jax.ShapeDtypeStruct((bsz * seq, d), BF16)]
    specs = [pl.BlockSpec((tm, d), lambda i, *_: (i, 0))]
    for _, r in DIL_PAIRS[1:]:
        shapes.append(jax.ShapeDtypeStruct((bsz, r, seq // r, d), BF16))
        specs.append(_residue_block(tm, seq, r, d))
    return shapes, specs


def _combine_ln(h, y_packed, pos_by_choice, wts, g, b, alpha, bsz, seq, emit_views):
    t, d = h.shape
    dp = y_packed.shape[1]
    tm = min(ROW_TM, t)
    row = pl.BlockSpec((tm, d), lambda i, pos: (i, 0))
    vec = pl.BlockSpec((1, d), lambda i, pos: (0, 0))
    v_shapes, v_specs = _view_shapes_specs(bsz, seq, d, tm) if emit_views else ([], [])
    scratch = [pltpu.VMEM((2, 2, tm, dp), jnp.uint32), pltpu.SemaphoreType.DMA((2, 2))]
    if emit_views:
        scratch.append(pltpu.VMEM((d // V7X_LANES, tm, V7X_LANES), F32))
    out = pl.pallas_call(
        functools.partial(_ln2_kernel, alpha=alpha, n_views=len(v_shapes)),
        out_shape=[jax.ShapeDtypeStruct((t, d), F32)] + v_shapes,
        grid_spec=pltpu.PrefetchScalarGridSpec(
            num_scalar_prefetch=1,
            grid=(t // tm,),
            in_specs=[row, pl.BlockSpec(memory_space=pl.ANY), pl.BlockSpec((tm, 2), lambda i, pos: (i, 0)), vec, vec],
            out_specs=[row] + v_specs,
            scratch_shapes=scratch),
        compiler_params=_params("arbitrary"),
        name="combine_ln",
    )(pos_by_choice, h, y_packed, wts, g.reshape(1, d), b.reshape(1, d))
    return out[0], tuple(out[1:])


def _views_kernel(x_ref, *rest):
    views, strip_ref = rest[:-1], rest[-1]
    x = x_ref[...]
    views[0][...] = x.astype(BF16)
    _emit_by_residue(x, strip_ref, views[1:])


def _bf16_views(x2d, bsz, seq):
    t, d = x2d.shape
    tm = min(ROW_TM, t)
    v_shapes, v_specs = _view_shapes_specs(bsz, seq, d, tm)
    return tuple(pl.pallas_call(
        _views_kernel,
        out_shape=v_shapes,
        grid=(t // tm,),
        in_specs=[pl.BlockSpec((tm, d), lambda i: (i, 0))],
        out_specs=v_specs,
        scratch_shapes=[pltpu.VMEM((d // V7X_LANES, tm, V7X_LANES), F32)],
        compiler_params=_params("parallel"),
        name="bf16_views",
    )(x2d))


def _layer(h, views, bsz, seq, alpha, last, w_in, b_gate, conv_w, conv_b, dt_bias, a_log, d_skip, ssm_norm_g,
           w_up_ssd, w_up_sb, w_up_dil, w_out, ln1_g, ln1_b, router_w, router_bias,
           w_gate_e, w_up_e, w_down_e, ln2_g, ln2_b):
    t = bsz * seq
    h_b = views[0]
    dil_w = 3 * DIL_WIDTH
    w_main = jnp.concatenate([w_in[:, :_IN_DT0], w_in[:, _IN_DT1:_IN_DIL0 + dil_w], w_in[:, _IN_GATE0:]],
                             axis=1).astype(BF16)
    w_dt = w_in[:, _IN_DT0:_IN_DT1].reshape(D_MODEL, SSD_GROUPS, SSD_HEADS_PER_GROUP)
    w_dt = jnp.pad(w_dt, ((0, 0), (0, 0), (0, DT_LANES - SSD_HEADS_PER_GROUP))).reshape(D_MODEL, DT_WIDTH)
    proj2d = _matmul(h_b, w_main, BF16, MM_TM, MM_TN)
    dt_raw = _matmul(h_b, w_dt.astype(BF16), F32, MM_TM, DT_WIDTH)
    proj = proj2d.reshape(bsz, seq, N_PROJ)

    y_ssd = _ssd(proj, dt_raw.reshape(bsz, seq, DT_WIDTH), conv_w, conv_b, dt_bias, a_log, d_skip, ssm_norm_g)
    y_sb = _stick_breaking(proj)

    assert DIL_PAIRS[0][1] == 1 and all(w // d == DIL_BLOCK for w, d in DIL_PAIRS)
    o, lse = _band_attention(proj, COL_DIL)
    dil = [o.reshape(t, DIL_WIDTH), lse.reshape(t, V7X_LANES)]
    for g, (_, r) in enumerate(DIL_PAIRS[1:], start=1):
        w_g = w_in[:, _IN_DIL0 + g * dil_w:_IN_DIL0 + (g + 1) * dil_w].astype(BF16)
        proj_g = _matmul(views[g].reshape(t, D_MODEL), w_g, BF16, MM_TM, MM_TN)
        o, lse = _band_attention(proj_g.reshape(bsz * r, seq // r, dil_w), 0)
        dil += [o.reshape(bsz, r, seq // r, DIL_WIDTH), lse.reshape(bsz, r, seq // r, V7X_LANES)]
    y_dil = _dilated_mixture(*dil, seq)

    merged = _merge_up(y_ssd.reshape(t, SSD_INNER), y_sb.reshape(t, SB_WIDTH), y_dil,
                       w_up_ssd.astype(BF16), w_up_sb.astype(BF16), w_up_dil.astype(BF16),
                       proj2d, b_gate.astype(F32).reshape(N_BRANCH, D_MODEL))
    h1, h1_p, meta, wts, counts = _out_ln(merged, w_out.astype(BF16), h, ln1_g, ln1_b, router_w, router_bias, alpha)

    src, pos_by_choice, tile_expert, n_used = _route_plan(meta, counts[:, 0].astype(jnp.int32), MOE_TM)
    y_p = _moe_experts(h1_p, src, tile_expert, n_used,
                       w_gate_e.astype(BF16), w_up_e.astype(BF16), w_down_e.astype(BF16))
    return _combine_ln(h1, y_p, pos_by_choice, wts.T, ln2_g, ln2_b, alpha, bsz, seq, emit_views=not last)


def kernel(x, w_in, b_gate, conv_w, conv_b, dt_bias, a_log, d_skip, ssm_norm_g, w_up_ssd, w_up_sb, w_up_dil,
           w_out, ln1_g, ln1_b, router_w, router_bias, w_gate_e, w_up_e, w_down_e, ln2_g, ln2_b):
    bsz, seq, d = x.shape
    depth = w_in.shape[0]
    alpha = (2 * depth) ** 0.25
    h = x.reshape(bsz * seq, d)
    views = _bf16_views(h, bsz, seq)
    for l in range(depth):
        h, views = _layer(h, views, bsz, seq, alpha, l == depth - 1, w_in[l], b_gate[l], conv_w[l], conv_b[l],
                          dt_bias[l], a_log[l], d_skip[l], ssm_norm_g[l], w_up_ssd[l], w_up_sb[l], w_up_dil[l],
                          w_out[l], ln1_g[l], ln1_b[l], router_w, router_bias, w_gate_e[l], w_up_e[l], w_down_e[l],
                          ln2_g[l], ln2_b[l])
    return h.reshape(bsz, seq, d)
```

```python
import functools

import jax
import jax.numpy as jnp
from jax import lax
from jax.experimental import pallas as pl
from jax.experimental.pallas import tpu as pltpu

F32 = jnp.float32
BF16 = jnp.bfloat16

D_MODEL = 2048
SSD_HEADS = 32
SSD_HEAD_DIM = 64
SSD_INNER = SSD_HEADS * SSD_HEAD_DIM
SSD_STATE = 128
SSD_GROUPS = 4
SSD_HEADS_PER_GROUP = SSD_HEADS // SSD_GROUPS
SSD_GROUP_WIDTH = SSD_HEADS_PER_GROUP * SSD_HEAD_DIM
SSD_CONV = 4
SSD_CHUNK = 256
SSD_BC = SSD_GROUPS * SSD_STATE
SB_HEADS = 16
SB_HEAD_DIM = 128
SB_WIDTH = SB_HEADS * SB_HEAD_DIM
DIL_PAIRS = ((128, 1), (512, 4), (2048, 16))
DIL_HEADS = 8
DIL_HEAD_DIM = 128
DIL_WIDTH = DIL_HEADS * DIL_HEAD_DIM
DIL_BLOCK = 128
N_BRANCH = 3
N_EXPERTS = 16
EXPERTS_PER_GROUP = 4
N_EXPERT_GROUPS = N_EXPERTS // EXPERTS_PER_GROUP
D_EXPERT = 1024
EPS = 1e-5
NEG = -1e30
LOG2E = 1.4426950408889634
F32_SUBNORMAL_EXP2 = -128.0

_IN_DT0 = SSD_INNER + SSD_INNER + 2 * SSD_BC
_IN_DT1 = _IN_DT0 + SSD_HEADS
_IN_END = _IN_DT1 + 3 * SB_WIDTH + 3 * len(DIL_PAIRS) * DIL_WIDTH + N_BRANCH * D_MODEL
COL_Z = 0
COL_X = SSD_INNER
COL_B = COL_X + SSD_INNER
COL_C = COL_B + SSD_BC
N_PROJ_A = COL_C + SSD_BC
WB_SB = 0
WB_DIL = WB_SB + 3 * SB_WIDTH
WB_GATE = WB_DIL + 3 * len(DIL_PAIRS) * DIL_WIDTH
WB_END = WB_GATE + N_BRANCH * D_MODEL
PB_SB = 0
PB_DIL = PB_SB + 3 * SB_WIDTH
PB_GATE = PB_DIL + 3 * DIL_WIDTH
N_PROJ_B = PB_GATE + N_BRANCH * D_MODEL
DT_LANES = 128
DT_WIDTH = SSD_GROUPS * DT_LANES

V7X_LANES = 128
V7X_VMEM_LIMIT = 48 * 1024 * 1024
MM_TM, MM_TN = 1024, 1024
CAST_TK = 1024
MERGE_TM, MERGE_TN = 512, 512
ROW_TM = 256
SB_BLOCK = 256
SB_HEADS_PER_STEP = 4
SB_EXP2_CLAMP = 64.0
MOE_TM = 256


def _params(*sem):
    return pltpu.CompilerParams(dimension_semantics=sem, vmem_limit_bytes=V7X_VMEM_LIMIT)


def _dot(a, b):
    return jnp.dot(a, b, preferred_element_type=F32)


def _dot_nt(a, b):
    return lax.dot_general(a, b, (((1,), (1,)), ((), ())), preferred_element_type=F32)


def _split_dot(x, m, passes):
    acc = None
    r = x
    for p in range(passes):
        s = r.astype(BF16)
        d = _dot(s, m)
        acc = d if acc is None else acc + d
        if p + 1 < passes:
            r = r - s.astype(F32)
    return acc


def _split_dot_left(m, x, passes):
    acc = None
    r = x
    for p in range(passes):
        s = r.astype(BF16)
        d = _dot(m, s)
        acc = d if acc is None else acc + d
        if p + 1 < passes:
            r = r - s.astype(F32)
    return acc


def _softplus(x):
    return jnp.maximum(x, 0.0) + jnp.log1p(jnp.exp(-jnp.abs(x)))


def _silu(x):
    return x * jax.nn.sigmoid(x)


def _layer_norm(y, g, b):
    mu = jnp.mean(y, axis=-1, keepdims=True)
    yc = y - mu
    var = jnp.mean(yc * yc, axis=-1, keepdims=True)
    return yc * lax.rsqrt(var + EPS) * g + b


def _pack_halves(x):
    n = x.shape[1] // 2
    bits = pltpu.bitcast(x.astype(BF16).astype(F32), jnp.uint32)
    return (bits[:, n:] & jnp.uint32(0xFFFF0000)) | (bits[:, :n] >> 16)


def _unpack_halves(p):
    lo = pltpu.bitcast(p << 16, F32)
    hi = pltpu.bitcast(p & jnp.uint32(0xFFFF0000), F32)
    return jnp.concatenate([lo, hi], axis=1)


def _start_row_gather(idx_ref, base, src_hbm, dst, sem):
    def issue(r, carry):
        pltpu.make_async_copy(src_hbm.at[pl.ds(idx_ref[base + r], 1)], dst.at[pl.ds(r, 1)], sem).start()
        return carry

    lax.fori_loop(0, dst.shape[0], issue, 0, unroll=8)


def _wait_row_gather(src_hbm, dst, sem):
    pltpu.make_async_copy(src_hbm.at[pl.ds(0, dst.shape[0])], dst, sem).wait()


def _expander(rows, cols, width):
    h = lax.broadcasted_iota(jnp.int32, (rows, cols), 0)
    j = lax.broadcasted_iota(jnp.int32, (rows, cols), 1)
    return jnp.where(h == j // width, 1.0, 0.0).astype(BF16)


def _mm_kernel(a_ref, w_ref, o_ref):
    o_ref[...] = _dot(a_ref[...], w_ref[...]).astype(o_ref.dtype)


def _matmul(a, w, out_dtype, tm, tn, col_ranges=None, layer=None):
    m, k = a.shape
    tm, tn = min(tm, m), min(tn, w.shape[-1])
    col_ranges = col_ranges or ((0, w.shape[-1]),)
    assert all(c0 % tn == 0 and c1 % tn == 0 for c0, c1 in col_ranges)
    n = sum(c1 - c0 for c0, c1 in col_ranges)

    def w_block(j):
        blk, first = None, 0
        for c0, c1 in col_ranges:
            here = c0 // tn + (j - first)
            blk = here if blk is None else jnp.where(j >= first, here, blk)
            first += (c1 - c0) // tn
        return blk

    return pl.pallas_call(
        _mm_kernel,
        out_shape=jax.ShapeDtypeStruct((m, n), out_dtype),
        grid=(m // tm, n // tn),
        in_specs=[pl.BlockSpec((tm, k), lambda i, j: (i, 0)),
                  pl.BlockSpec((k, tn), lambda i, j: (0, w_block(j))) if layer is None else
                  pl.BlockSpec((None, k, tn), lambda i, j: (layer, 0, w_block(j)))],
        out_specs=pl.BlockSpec((tm, tn), lambda i, j: (i, j)),
        compiler_params=_params("parallel", "parallel"),
        name="matmul",
    )(a, w)


def _cast_cols_kernel(a_ref, b_ref, o_ref, *, shift):
    a = a_ref[0]
    if shift:
        a = jnp.concatenate([a, b_ref[0]], axis=1)[:, shift:shift + a.shape[1]]
    o_ref[0] = a.astype(o_ref.dtype)


def _cast_cols(w, col0, n_out):
    depth, k, n_in = w.shape
    tk, tn = CAST_TK, MM_TN
    shift = col0 % V7X_LANES
    base = col0 - shift
    assert base % tn == 0 and n_out % tn == 0 and k % tk == 0 and col0 + n_out <= n_in
    return pl.pallas_call(
        functools.partial(_cast_cols_kernel, shift=shift),
        out_shape=jax.ShapeDtypeStruct((depth, k, n_out), BF16),
        grid=(depth, k // tk, n_out // tn),
        in_specs=[pl.BlockSpec((1, tk, tn), lambda l, i, j: (l, i, base // tn + j)),
                  pl.BlockSpec((1, tk, V7X_LANES), lambda l, i, j: (l, i, (base + tn * (j + 1)) // V7X_LANES))],
        out_specs=pl.BlockSpec((1, tk, tn), lambda l, i, j: (l, i, j)),
        compiler_params=_params("parallel", "parallel", "parallel"),
        name="cast_cols",
    )(w, w)


def _ssd_kernel(z_ref, x_ref, b_ref, c_ref, dt_ref, cwx_ref, cwb_ref, cwc_ref, cbx_ref, cbb_ref, cbc_ref,
                dtb_ref, alog_ref, dskip_ref, g_ref, o_ref,
                state_ref, xpad_ref, bpad_ref, cpad_ref, xs_ref, bs_ref, cs_ref, y_ref,
                dts_ref, acs_ref, ecs_ref, dte_ref):
    chunk = x_ref.shape[1]
    pad = 8

    @pl.when(pl.program_id(1) == 0)
    def _():
        state_ref[...] = jnp.zeros_like(state_ref)
        xpad_ref[0:pad, :] = jnp.zeros((pad, xpad_ref.shape[1]), F32)
        bpad_ref[0:pad, :] = jnp.zeros((pad, bpad_ref.shape[1]), F32)
        cpad_ref[0:pad, :] = jnp.zeros((pad, cpad_ref.shape[1]), F32)

    def conv_silu(src_ref, pad_ref, cw_ref, cb_ref, dst_ref):
        pad_ref[pad:pad + chunk, :] = src_ref[0].astype(F32)
        acc = cb_ref[...]
        for k in range(SSD_CONV):
            off = pad - (SSD_CONV - 1) + k
            acc = acc + cw_ref[k:k + 1, :] * pad_ref[off:off + chunk, :]
        dst_ref[...] = _silu(acc)
        pad_ref[0:pad, :] = pad_ref[chunk:chunk + pad, :]

    conv_silu(x_ref, xpad_ref, cwx_ref, cbx_ref, xs_ref)
    conv_silu(b_ref, bpad_ref, cwb_ref, cbb_ref, bs_ref)
    conv_silu(c_ref, cpad_ref, cwc_ref, cbc_ref, cs_ref)

    dt = _softplus(dt_ref[0] + dtb_ref[...])
    a_dt = dt * (-jnp.exp(alog_ref[...]))
    row = lax.broadcasted_iota(jnp.int32, (chunk, chunk), 0)
    col = lax.broadcasted_iota(jnp.int32, (chunk, chunk), 1)
    causal = col <= row
    tri = jnp.where(causal, 1.0, 0.0).astype(BF16)
    a_cs = _split_dot_left(tri, a_dt, 3)
    last = a_cs[chunk - 1:chunk, :]
    dts_ref[...] = dt
    acs_ref[...] = a_cs
    ecs_ref[...] = jnp.exp(a_cs)
    dte_ref[...] = jnp.exp(last - a_cs)

    expand = _expander(DT_LANES, SSD_GROUP_WIDTH, SSD_HEAD_DIM)
    lane = lax.broadcasted_iota(jnp.int32, (chunk, V7X_LANES), 1)
    heads_per_tile = V7X_LANES // SSD_HEAD_DIM

    def group(g, carry):
        ch0 = pl.multiple_of(g * SSD_GROUP_WIDTH, SSD_GROUP_WIDTH)
        st0 = pl.multiple_of(g * SSD_STATE, SSD_STATE)
        dt0 = pl.multiple_of(g * DT_LANES, DT_LANES)
        xg = xs_ref[:, pl.ds(ch0, SSD_GROUP_WIDTH)]
        bg = bs_ref[:, pl.ds(st0, SSD_STATE)]
        cg = cs_ref[:, pl.ds(st0, SSD_STATE)].astype(BF16)
        csg = acs_ref[:, pl.ds(dt0, DT_LANES)]
        dt_x = _split_dot(dts_ref[:, pl.ds(dt0, DT_LANES)], expand, 2)
        ecs_x = _split_dot(ecs_ref[:, pl.ds(dt0, DT_LANES)], expand, 2)
        dte_x = _split_dot(dte_ref[:, pl.ds(dt0, DT_LANES)], expand, 2)
        xdt = xg * dt_x
        xdt_b = xdt.astype(BF16)
        cb = _dot_nt(cg, bg.astype(BF16))
        cs_t = csg.T
        prev = state_ref[g]
        y_off = _dot(cg, prev.astype(BF16)) * ecs_x
        for t in range(SSD_GROUP_WIDTH // V7X_LANES):
            lanes = slice(t * V7X_LANES, (t + 1) * V7X_LANES)
            xdt_t = xdt_b[:, lanes]
            y_t = None
            for u in range(heads_per_tile):
                r = t * heads_per_tile + u
                seg = csg[:, r:r + 1] - cs_t[r:r + 1, :]
                m = (cb * jnp.exp(jnp.where(causal, seg, NEG))).astype(BF16)
                y_r = _dot(m, xdt_t)
                y_t = y_r if y_t is None else jnp.where(lane // SSD_HEAD_DIM == u, y_r, y_t)
            c0 = pl.multiple_of(ch0 + t * V7X_LANES, V7X_LANES)
            y_ref[:, pl.ds(c0, V7X_LANES)] = (y_t + y_off[:, lanes]
                                              + dskip_ref[:, pl.ds(c0, V7X_LANES)] * xg[:, lanes])
        new = ecs_x[chunk - 1:chunk, :] * prev + _dot(bg.T.astype(BF16), (xdt * dte_x).astype(BF16))
        state_ref[g] = new
        return carry

    lax.fori_loop(0, SSD_GROUPS, group, 0)

    yf = y_ref[...] * _silu(z_ref[0].astype(F32))
    ms = jnp.mean(yf * yf, axis=-1, keepdims=True)
    o_ref[0] = (yf * lax.rsqrt(ms + EPS) * g_ref[...]).astype(o_ref.dtype)


def _ssd(proj, dt_raw, conv_w, conv_b, dt_bias, a_log, d_skip, norm_g):
    bsz, seq, _ = proj.shape
    chunk = min(SSD_CHUNK, seq)
    nc = seq // chunk

    def spread_heads(v):
        v = v.astype(F32).reshape(SSD_GROUPS, SSD_HEADS_PER_GROUP)
        return jnp.pad(v, ((0, 0), (0, DT_LANES - SSD_HEADS_PER_GROUP))).reshape(1, DT_WIDTH)

    cw = conv_w.astype(F32)
    cbias = conv_b.astype(F32).reshape(1, -1)
    x1, b1 = SSD_INNER, SSD_INNER + SSD_BC
    consts = [cw[:, :x1], cw[:, x1:b1], cw[:, b1:], cbias[:, :x1], cbias[:, x1:b1], cbias[:, b1:],
              spread_heads(dt_bias), spread_heads(a_log),
              jnp.repeat(d_skip.astype(F32), SSD_HEAD_DIM).reshape(1, SSD_INNER),
              norm_g.astype(F32).reshape(1, SSD_INNER)]

    def col_block(width, col):
        return pl.BlockSpec((1, chunk, width), lambda b, c: (b, c, col // width))

    def whole(a):
        return pl.BlockSpec(a.shape, lambda b, c: (0, 0))

    return pl.pallas_call(
        _ssd_kernel,
        out_shape=jax.ShapeDtypeStruct((bsz, seq, SSD_INNER), BF16),
        grid=(bsz, nc),
        in_specs=[col_block(SSD_INNER, COL_Z), col_block(SSD_INNER, COL_X),
                  col_block(SSD_BC, COL_B), col_block(SSD_BC, COL_C),
                  pl.BlockSpec((1, chunk, DT_WIDTH), lambda b, c: (b, c, 0))] + [whole(a) for a in consts],
        out_specs=pl.BlockSpec((1, chunk, SSD_INNER), lambda b, c: (b, c, 0)),
        scratch_shapes=[pltpu.VMEM((SSD_GROUPS, SSD_STATE, SSD_GROUP_WIDTH), F32),
                        pltpu.VMEM((chunk + 8, SSD_INNER), F32),
                        pltpu.VMEM((chunk + 8, SSD_BC), F32),
                        pltpu.VMEM((chunk + 8, SSD_BC), F32),
                        pltpu.VMEM((chunk, SSD_INNER), F32),
                        pltpu.VMEM((chunk, SSD_BC), F32),
                        pltpu.VMEM((chunk, SSD_BC), F32),
                        pltpu.VMEM((chunk, SSD_INNER), F32),
                        pltpu.VMEM((chunk, DT_WIDTH), F32),
                        pltpu.VMEM((chunk, DT_WIDTH), F32),
                        pltpu.VMEM((chunk, DT_WIDTH), F32),
                        pltpu.VMEM((chunk, DT_WIDTH), F32)],
        compiler_params=_params("parallel", "arbitrary"),
        name="ssd",
    )(proj, proj, proj, proj, dt_raw, *consts)


def _sb_kernel(q_ref, k_ref, v_ref, o_ref, *, blk, scale2):
    qi = pl.program_id(2)
    hd = SB_HEAD_DIM
    row = lax.broadcasted_iota(jnp.int32, (blk, blk), 0)
    col = lax.broadcasted_iota(jnp.int32, (blk, blk), 1)
    strict = col < row
    upper = jnp.where(row >= col, 1.0, 0.0).astype(BF16)
    upper2 = jnp.concatenate([upper, upper], axis=0)

    lanes = [slice(h * hd, (h + 1) * hd) for h in range(q_ref.shape[2] // hd)]

    def blocks(kb, rs, diagonal):
        start = pl.multiple_of(kb * blk, blk)
        zs = [_dot_nt(q_ref[0, :, sl], k_ref[0, pl.ds(start, blk), sl]) * scale2 for sl in lanes]
        sufs = []
        for z in zs:
            sp = jnp.maximum(z, jnp.log2(1.0 + jnp.exp2(jnp.minimum(z, SB_EXP2_CLAMP))))
            if diagonal:
                sp = jnp.where(strict, sp, 0.0)
            hi = pltpu.bitcast(pltpu.bitcast(sp, jnp.uint32) & jnp.uint32(0xFFFF0000), F32)
            parts = jnp.concatenate([hi.astype(BF16), (sp - hi).astype(BF16)], axis=1)
            sufs.append(_dot(parts, upper2))
        new_rs = tuple(r - suf[:, 0:1] for r, suf in zip(rs, sufs))
        pvs = []
        for z, suf, r, sl in zip(zs, sufs, rs, lanes):
            w = jnp.exp2(z - suf + r)
            if diagonal:
                w = jnp.where(strict, w, 0.0)
            pvs.append(_dot(w.astype(BF16), v_ref[0, pl.ds(start, blk), sl]))
        return new_rs, pvs

    def largest(rs):
        m = rs[0]
        for r in rs[1:]:
            m = jnp.maximum(m, r)
        return jnp.max(m)

    rs, accs = blocks(qi, tuple(jnp.zeros((blk, 1), F32) for _ in lanes), True)

    def more(c):
        return (c[0] >= 0) & (c[1] > F32_SUBNORMAL_EXP2)

    def body(c):
        kb, _, rs, accs = c
        rs, pvs = blocks(kb, rs, False)
        return kb - 1, largest(rs), rs, tuple(a + pv for a, pv in zip(accs, pvs))

    _, _, _, accs = lax.while_loop(more, body, (qi - 1, largest(rs), rs, tuple(accs)))
    o_ref[0] = jnp.concatenate(accs, axis=1).astype(o_ref.dtype)


def _stick_breaking(proj):
    bsz, seq, _ = proj.shape
    blk = min(SB_BLOCK, seq)
    qc, kc, vc = (PB_SB + i * SB_WIDTH for i in range(3))
    w = SB_HEADS_PER_STEP * SB_HEAD_DIM
    return pl.pallas_call(
        functools.partial(_sb_kernel, blk=blk, scale2=SB_HEAD_DIM ** -0.5 * LOG2E),
        out_shape=jax.ShapeDtypeStruct((bsz, seq, SB_WIDTH), BF16),
        grid=(bsz, SB_WIDTH // w, seq // blk),
        in_specs=[pl.BlockSpec((1, blk, w), lambda b, h, i: (b, i, qc // w + h)),
                  pl.BlockSpec((1, seq, w), lambda b, h, i: (b, 0, kc // w + h)),
                  pl.BlockSpec((1, seq, w), lambda b, h, i: (b, 0, vc // w + h))],
        out_specs=pl.BlockSpec((1, blk, w), lambda b, h, i: (b, i, h)),
        compiler_params=_params("parallel", "parallel", "arbitrary"),
        name="stick_breaking",
    )(proj, proj, proj)


def _dil_kernel(q_ref, kp_ref, kc_ref, vp_ref, vc_ref, o_ref, lse_ref, *, scale):
    bq = DIL_BLOCK
    blk = pl.program_id(1)
    i = lax.broadcasted_iota(jnp.int32, (bq, 2 * bq), 0)
    m = lax.broadcasted_iota(jnp.int32, (bq, 2 * bq), 1)
    dist = i - m + bq
    first = jnp.where(blk > 0, 0, bq)
    valid = (dist >= 0) & (dist <= bq) & (m >= first)
    lane = lax.broadcasted_iota(jnp.int32, (bq, V7X_LANES), 1)
    lse_all = jnp.zeros((bq, V7X_LANES), F32)
    for h in range(DIL_HEADS):
        sl = slice(h * DIL_HEAD_DIM, (h + 1) * DIL_HEAD_DIM)
        k = jnp.concatenate([kp_ref[0, :, sl], kc_ref[0, :, sl]], axis=0)
        v = jnp.concatenate([vp_ref[0, :, sl], vc_ref[0, :, sl]], axis=0)
        s = jnp.where(valid, _dot_nt(q_ref[0, :, sl], k) * scale, NEG)
        mx = jnp.max(s, axis=-1, keepdims=True)
        p = jnp.exp(s - mx)
        den = jnp.sum(p, axis=-1, keepdims=True)
        o_ref[0, :, sl] = (_dot(p.astype(BF16), v) / den).astype(o_ref.dtype)
        lse_all = jnp.where(lane == h, mx + jnp.log(den), lse_all)
    lse_ref[0] = lse_all


def _band_attention(qkv, col):
    nz, lc, _ = qkv.shape
    bq = DIL_BLOCK
    w = DIL_WIDTH
    assert lc % bq == 0 and col % w == 0

    def cur(part):
        return pl.BlockSpec((1, bq, w), lambda z, i: (z, i, col // w + part))

    def prev(part):
        return pl.BlockSpec((1, bq, w), lambda z, i: (z, jnp.maximum(i - 1, 0), col // w + part))

    return pl.pallas_call(
        functools.partial(_dil_kernel, scale=DIL_HEAD_DIM ** -0.5),
        out_shape=(jax.ShapeDtypeStruct((nz, lc, w), BF16),
                   jax.ShapeDtypeStruct((nz, lc, V7X_LANES), F32)),
        grid=(nz, lc // bq),
        in_specs=[cur(0), prev(1), cur(1), prev(2), cur(2)],
        out_specs=(pl.BlockSpec((1, bq, w), lambda z, i: (z, i, 0)),
                   pl.BlockSpec((1, bq, V7X_LANES), lambda z, i: (z, i, 0))),
        compiler_params=_params("parallel", "arbitrary"),
        name="dilated_attention",
    )(qkv, qkv, qkv, qkv, qkv)


def _residue_block(tm, seq, dilation, width):
    tiles = seq // tm
    return pl.BlockSpec((1, dilation, tm // dilation, width), lambda i, *_: (i // tiles, 0, i % tiles, 0))


def _emit_by_residue(x, strip_ref, dst_refs):
    tm, w = x.shape
    strips = w // V7X_LANES
    for s in range(strips):
        strip_ref[s] = x[:, s * V7X_LANES:(s + 1) * V7X_LANES]
    for dst in dst_refs:
        d, n = dst.shape[1], dst.shape[2]
        for c in range(d):
            for s in range(strips):
                dst[0, c, :, s * V7X_LANES:(s + 1) * V7X_LANES] = strip_ref[s, pl.ds(c, n, stride=d), :].astype(dst.dtype)


def _dilmix_kernel(o0_ref, l0_ref, o1_ref, l1_ref, o2_ref, l2_ref, out_ref, strip_ref, lse_ref):
    tm = out_ref.shape[0]
    heads = DIL_WIDTH // DIL_HEAD_DIM
    for gi, (o_ref, l_ref) in enumerate(((o1_ref, l1_ref), (o2_ref, l2_ref))):
        d = o_ref.shape[1]
        n = tm // d
        for c in range(d):
            lse_ref[gi, pl.ds(c, n, stride=d), :] = l_ref[0, c]
            for j in range(heads):
                strip_ref[gi, j, pl.ds(c, n, stride=d), :] = (
                    o_ref[0, c, :, j * DIL_HEAD_DIM:(j + 1) * DIL_HEAD_DIM].astype(F32))
    ls = [l0_ref[...], lse_ref[0], lse_ref[1]]
    mx = jnp.maximum(jnp.maximum(ls[0], ls[1]), ls[2])
    es = [jnp.exp(l - mx) for l in ls]
    inv = 1.0 / (es[0] + es[1] + es[2])
    ws = [e * inv for e in es]
    for j in range(heads):
        sl = slice(j * DIL_HEAD_DIM, (j + 1) * DIL_HEAD_DIM)
        acc = (ws[0][:, j:j + 1] * o0_ref[:, sl].astype(F32) + ws[1][:, j:j + 1] * strip_ref[0, j]
               + ws[2][:, j:j + 1] * strip_ref[1, j])
        out_ref[:, sl] = acc.astype(out_ref.dtype)


def _dilated_mixture(o0, l0, o1, l1, o2, l2, seq):
    t = o0.shape[0]
    tm = min(ROW_TM, t)
    heads = DIL_WIDTH // DIL_HEAD_DIM
    assert DIL_HEAD_DIM == V7X_LANES
    o_spec = pl.BlockSpec((tm, DIL_WIDTH), lambda i: (i, 0))
    l_spec = pl.BlockSpec((tm, V7X_LANES), lambda i: (i, 0))
    res = [spec for o in (o1, o2) for spec in (_residue_block(tm, seq, o.shape[1], DIL_WIDTH),
                                               _residue_block(tm, seq, o.shape[1], V7X_LANES))]
    return pl.pallas_call(
        _dilmix_kernel,
        out_shape=jax.ShapeDtypeStruct((t, DIL_WIDTH), BF16),
        grid=(t // tm,),
        in_specs=[o_spec, l_spec] + res,
        out_specs=o_spec,
        scratch_shapes=[pltpu.VMEM((2, heads, tm, DIL_HEAD_DIM), F32), pltpu.VMEM((2, tm, V7X_LANES), F32)],
        compiler_params=_params("parallel"),
        name="dilated_mix",
    )(o0, l0, o1, l1, o2, l2)


def _merge_kernel(ya_ref, yb_ref, yc_ref, ua_ref, ub_ref, uc_ref, ga_ref, gb_ref, gc_ref, bg_ref, o_ref):
    acc = None
    for i, (y_ref, u_ref, g_ref) in enumerate(((ya_ref, ua_ref, ga_ref), (yb_ref, ub_ref, gb_ref),
                                               (yc_ref, uc_ref, gc_ref))):
        gate = jax.nn.sigmoid(g_ref[...].astype(F32) + bg_ref[i:i + 1, :])
        term = gate * _dot(y_ref[...], u_ref[...])
        acc = term if acc is None else acc + term
    o_ref[...] = acc.astype(o_ref.dtype)


def _merge_up(y_ssd, y_sb, y_dil, layer, u_ssd, u_sb, u_dil, proj2d, b_gate):
    t = y_ssd.shape[0]
    tm, tn = min(MERGE_TM, t), MERGE_TN

    def rows(a):
        return pl.BlockSpec((tm, a.shape[1]), lambda i, j: (i, 0))

    def cols(a):
        return pl.BlockSpec((None, a.shape[1], tn), lambda i, j: (layer, 0, j))

    def gate(k):
        return pl.BlockSpec((tm, tn), lambda i, j: (i, (PB_GATE + k * D_MODEL) // tn + j))

    return pl.pallas_call(
        _merge_kernel,
        out_shape=jax.ShapeDtypeStruct((t, D_MODEL), BF16),
        grid=(t // tm, D_MODEL // tn),
        in_specs=[rows(y_ssd), rows(y_sb), rows(y_dil), cols(u_ssd), cols(u_sb), cols(u_dil),
                  gate(0), gate(1), gate(2), pl.BlockSpec((N_BRANCH, tn), lambda i, j: (0, j))],
        out_specs=pl.BlockSpec((tm, tn), lambda i, j: (i, j)),
        compiler_params=_params("parallel", "parallel"),
        name="merge_up",
    )(y_ssd, y_sb, y_dil, u_ssd, u_sb, u_dil, proj2d, proj2d, proj2d, b_gate)


def _outln_kernel(m_ref, w_ref, h_ref, g_ref, b_ref, rwh_ref, rwl_ref, rb_ref, of_ref, op_ref, meta_ref, wts_ref,
                  cnt_ref, *, alpha):
    @pl.when(pl.program_id(0) == 0)
    def _():
        cnt_ref[...] = jnp.zeros_like(cnt_ref)

    y = alpha * h_ref[...] + _dot(m_ref[...], w_ref[...])
    out = _layer_norm(y, g_ref[...], b_ref[...])
    of_ref[...] = out
    op_ref[...] = _pack_halves(out)
    e1, e2, w1, w2 = _route(out, rwh_ref[...], rwl_ref[...], rb_ref[...])
    r1, r2 = _expert_ranks(e1, e2, cnt_ref)
    for row, v in enumerate((e1, e2, r1, r2)):
        meta_ref[row:row + 1, :] = v
    wts_ref[0:1, :] = w1
    wts_ref[1:2, :] = w2


def _out_ln(merged, layer, w_out, h, g, b, router_w, router_bias, alpha):
    t, d = h.shape
    tm = min(ROW_TM, t)
    row = pl.BlockSpec((tm, d), lambda i: (i, 0))
    vec = pl.BlockSpec((1, d), lambda i: (0, 0))
    rw = jnp.pad(router_w.astype(F32), ((0, 0), (0, V7X_LANES - N_EXPERTS)))
    rw_hi = rw.astype(BF16)
    rw_lo = (rw - rw_hi.astype(F32)).astype(BF16)
    rw_spec = pl.BlockSpec((d, V7X_LANES), lambda i: (0, 0))
    return pl.pallas_call(
        functools.partial(_outln_kernel, alpha=alpha),
        out_shape=(jax.ShapeDtypeStruct((t, d), F32), jax.ShapeDtypeStruct((t, d // 2), jnp.uint32),
                   jax.ShapeDtypeStruct((4, t), jnp.int32), jax.ShapeDtypeStruct((2, t), F32),
                   jax.ShapeDtypeStruct((N_EXPERTS, V7X_LANES), F32)),
        grid=(t // tm,),
        in_specs=[row, pl.BlockSpec((None, d, d), lambda i: (layer, 0, 0)), row, vec, vec,
                  rw_spec, rw_spec, pl.BlockSpec((N_EXPERTS, 1), lambda i: (0, 0))],
        out_specs=(row, pl.BlockSpec((tm, d // 2), lambda i: (i, 0)),
                   pl.BlockSpec((4, tm), lambda i: (0, i)), pl.BlockSpec((2, tm), lambda i: (0, i)),
                   pl.BlockSpec((N_EXPERTS, V7X_LANES), lambda i: (0, 0))),
        compiler_params=_params("arbitrary"),
        name="out_ln",
    )(merged, w_out, h, g.reshape(1, d), b.reshape(1, d),
      rw_hi, rw_lo, router_bias.astype(F32).reshape(N_EXPERTS, 1))


def _route(h, rw_hi, rw_lo, rb):
    h_hi = h.astype(BF16)
    h_lo = (h - h_hi.astype(F32)).astype(BF16)
    logits = _dot(h_hi, rw_hi) + (_dot(h_lo, rw_hi) + _dot(h_hi, rw_lo))
    logits = logits.T[0:N_EXPERTS, :]
    mx = jnp.max(logits, axis=0, keepdims=True)
    ex = jnp.exp(logits - mx)
    aff = ex / jnp.sum(ex, axis=0, keepdims=True)
    sel = aff + rb
    s = [sel[e:e + 1, :] for e in range(N_EXPERTS)]
    a = [aff[e:e + 1, :] for e in range(N_EXPERTS)]
    n = EXPERTS_PER_GROUP
    best_score, best = None, None
    for g in range(N_EXPERT_GROUPS):
        grp = s[g * n:(g + 1) * n]
        score = None
        for i in range(n):
            for j in range(i + 1, n):
                pair = grp[i] + grp[j]
                score = pair if score is None else jnp.maximum(score, pair)
        if best is None:
            best_score, best = score, jnp.zeros_like(score, dtype=jnp.int32)
        else:
            take = score > best_score
            best_score = jnp.where(take, score, best_score)
            best = jnp.where(take, g, best)
    sv, av = [], []
    for r in range(n):
        sr, ar = s[r], a[r]
        for g in range(1, N_EXPERT_GROUPS):
            sr = jnp.where(best == g, s[g * n + r], sr)
            ar = jnp.where(best == g, a[g * n + r], ar)
        sv.append(sr)
        av.append(ar)

    def arg_top(vals, skip):
        top_v, top_i, top_a = None, None, None
        for r in range(n):
            v = vals[r] if skip is None else jnp.where(skip == r, -jnp.inf, vals[r])
            if top_v is None:
                top_v, top_i, top_a = v, jnp.zeros_like(best), av[0]
            else:
                take = v > top_v
                top_v = jnp.where(take, v, top_v)
                top_i = jnp.where(take, r, top_i)
                top_a = jnp.where(take, av[r], top_a)
        return top_i, top_a

    i1, a1 = arg_top(sv, None)
    i2, a2 = arg_top(sv, i1)
    tot = a1 + a2
    return best * n + i1, best * n + i2, a1 / tot, a2 / tot


def _expert_ranks(e1, e2, count_ref):
    tm = e1.shape[1]
    expert = lax.broadcasted_iota(jnp.int32, (N_EXPERTS, tm), 0)
    hit1, hit2 = expert == e1, expert == e2
    both = jnp.where(hit1, 1.0, 0.0) + jnp.where(hit2, 1.0, 0.0)
    earlier = lax.broadcasted_iota(jnp.int32, (tm, tm), 0) < lax.broadcasted_iota(jnp.int32, (tm, tm), 1)
    before = _dot(both.astype(BF16), jnp.where(earlier, 1.0, 0.0).astype(BF16)) + count_ref[:, 0:1]
    r1 = jnp.sum(jnp.where(hit1, before, 0.0), axis=0, keepdims=True)
    r2 = jnp.sum(jnp.where(hit2, before, 0.0), axis=0, keepdims=True)
    count_ref[...] = count_ref[...] + jnp.sum(both, axis=1, keepdims=True)
    return r1.astype(jnp.int32), r2.astype(jnp.int32)


def _moe_kernel(te_ref, nu_ref, fill_ref, pos_ref, x_hbm, wg_ref, wu_ref, wd_ref, o_ref, xbuf, sem, src_ref):
    i = pl.program_id(0)
    n_used = nu_ref[0]
    tm = xbuf.shape[1]
    n_tokens = pos_ref.shape[0] // 2

    @pl.when(i == 0)
    def _():
        def pad_expert(e, carry):
            def pad_slot(s, c):
                src_ref[s] = 0
                return c

            return lax.fori_loop(fill_ref[e], fill_ref[N_EXPERTS + e], pad_slot, carry)

        lax.fori_loop(0, N_EXPERTS, pad_expert, 0)
        for k in range(2):
            def place(tok, carry):
                src_ref[pos_ref[k * n_tokens + tok]] = tok
                return carry

            lax.fori_loop(0, n_tokens, place, 0, unroll=8)

    def fetch(tile):
        slot = tile % 2
        _start_row_gather(src_ref, tile * tm, x_hbm, xbuf.at[slot], sem.at[slot])

    @pl.when(i == 0)
    def _():
        fetch(i)

    @pl.when(i + 1 < n_used)
    def _():
        fetch(i + 1)

    @pl.when(i < n_used)
    def _():
        slot = i % 2
        _wait_row_gather(x_hbm, xbuf.at[slot], sem.at[slot])
        x = _unpack_halves(xbuf[slot]).astype(BF16)
        act = _silu(_dot(x, wg_ref[...])) * _dot(x, wu_ref[...])
        o_ref[...] = _pack_halves(_dot(act.astype(BF16), wd_ref[...]))

    @pl.when(i >= n_used)
    def _():
        o_ref[...] = jnp.zeros_like(o_ref)


def _route_plan(meta, counts, tm):
    t = meta.shape[1]
    padded = (counts + tm - 1) // tm * tm
    ends = jnp.cumsum(padded)
    starts = ends - padded
    one_hot = meta[0:2, :, None] == jnp.arange(N_EXPERTS, dtype=jnp.int32)
    pos = jnp.sum(jnp.where(one_hot, starts, 0), axis=-1) + meta[2:4]
    n_tiles = (2 * t + N_EXPERTS * tm) // tm
    n_used = (ends[-1] // tm).astype(jnp.int32)
    tile_start = jnp.minimum(jnp.arange(n_tiles, dtype=jnp.int32), n_used - 1) * tm
    owner = jnp.sum((ends[None, :] <= tile_start[:, None]).astype(jnp.int32), axis=1)
    fill = jnp.concatenate([starts + counts, ends]).astype(jnp.int32)
    return pos.reshape(-1), jnp.minimum(owner, N_EXPERTS - 1), n_used.reshape(1), fill


def _moe_experts(x_packed, pos, tile_expert, n_used, fill, layer, w_gate, w_up, w_down):
    n_tiles = tile_expert.shape[0]
    dp = x_packed.shape[1]
    tm = MOE_TM
    d, de = w_gate.shape[2], w_gate.shape[3]

    def weights(i, te, *_):
        return (layer, te[i], 0, 0)

    return pl.pallas_call(
        _moe_kernel,
        out_shape=jax.ShapeDtypeStruct((n_tiles * tm, dp), jnp.uint32),
        grid_spec=pltpu.PrefetchScalarGridSpec(
            num_scalar_prefetch=4,
            grid=(n_tiles,),
            in_specs=[pl.BlockSpec(memory_space=pl.ANY),
                      pl.BlockSpec((None, None, d, de), weights),
                      pl.BlockSpec((None, None, d, de), weights),
                      pl.BlockSpec((None, None, de, d), weights)],
            out_specs=pl.BlockSpec((tm, dp), lambda i, *_: (i, 0)),
            scratch_shapes=[pltpu.VMEM((2, tm, dp), jnp.uint32), pltpu.SemaphoreType.DMA((2,)),
                            pltpu.SMEM((n_tiles * tm,), jnp.int32)]),
        compiler_params=_params("arbitrary"),
        name="moe_experts",
    )(tile_expert, n_used, fill, pos, x_packed, w_gate, w_up, w_down)


def _ln2_kernel(pos_ref, h_ref, y_hbm, w_ref, g_ref, b_ref, of_ref, *rest, alpha, n_views):
    views, (rbuf, sem), strips = rest[:n_views], rest[n_views:n_views + 2], rest[n_views + 2:]
    i = pl.program_id(0)
    n_tiles = pl.num_programs(0)
    tm = h_ref.shape[0]

    def fetch(tile):
        slot = tile % 2
        for k in range(2):
            _start_row_gather(pos_ref, (k * n_tiles + tile) * tm, y_hbm, rbuf.at[slot, k], sem.at[slot, k])

    @pl.when(i == 0)
    def _():
        fetch(i)

    @pl.when(i + 1 < n_tiles)
    def _():
        fetch(i + 1)

    slot = i % 2
    y = alpha * h_ref[...]
    w = w_ref[...]
    for k in range(2):
        _wait_row_gather(y_hbm, rbuf.at[slot, k], sem.at[slot, k])
        y = y + w[:, k:k + 1] * _unpack_halves(rbuf[slot, k])
    out = _layer_norm(y, g_ref[...], b_ref[...])
    of_ref[...] = out
    if views:
        views[0][...] = out.astype(BF16)
        _emit_by_residue(out, strips[0], views[1:])


def _view_shapes_specs(bsz, seq, d, tm):
    shapes = [jax.ShapeDtypeStruct((bsz * seq, d), BF16)]
    specs = [pl.BlockSpec((tm, d), lambda i, *_: (i, 0))]
    for _, r in DIL_PAIRS[1:]:
        shapes.append(jax.ShapeDtypeStruct((bsz, r, seq // r, d), BF16))
        specs.append(_residue_block(tm, seq, r, d))
    return shapes, specs


def _combine_ln(h, y_packed, pos_by_choice, wts, g, b, alpha, bsz, seq, emit_views):
    t, d = h.shape
    dp = y_packed.shape[1]
    tm = min(ROW_TM, t)
    row = pl.BlockSpec((tm, d), lambda i, pos: (i, 0))
    vec = pl.BlockSpec((1, d), lambda i, pos: (0, 0))
    v_shapes, v_specs = _view_shapes_specs(bsz, seq, d, tm) if emit_views else ([], [])
    scratch = [pltpu.VMEM((2, 2, tm, dp), jnp.uint32), pltpu.SemaphoreType.DMA((2, 2))]
    if emit_views:
        scratch.append(pltpu.VMEM((d // V7X_LANES, tm, V7X_LANES), F32))
    out = pl.pallas_call(
        functools.partial(_ln2_kernel, alpha=alpha, n_views=len(v_shapes)),
        out_shape=[jax.ShapeDtypeStruct((t, d), F32)] + v_shapes,
        grid_spec=pltpu.PrefetchScalarGridSpec(
            num_scalar_prefetch=1,
            grid=(t // tm,),
            in_specs=[row, pl.BlockSpec(memory_space=pl.ANY), pl.BlockSpec((tm, 2), lambda i, pos: (i, 0)), vec, vec],
            out_specs=[row] + v_specs,
            scratch_shapes=scratch),
        compiler_params=_params("arbitrary"),
        name="combine_ln",
    )(pos_by_choice, h, y_packed, wts, g.reshape(1, d), b.reshape(1, d))
    return out[0], tuple(out[1:])


def _views_kernel(x_ref, *rest):
    views, strip_ref = rest[:-1], rest[-1]
    x = x_ref[...]
    views[0][...] = x.astype(BF16)
    _emit_by_residue(x, strip_ref, views[1:])


def _bf16_views(x2d, bsz, seq):
    t, d = x2d.shape
    tm = min(ROW_TM, t)
    v_shapes, v_specs = _view_shapes_specs(bsz, seq, d, tm)
    return tuple(pl.pallas_call(
        _views_kernel,
        out_shape=v_shapes,
        grid=(t // tm,),
        in_specs=[pl.BlockSpec((tm, d), lambda i: (i, 0))],
        out_specs=v_specs,
        scratch_shapes=[pltpu.VMEM((d // V7X_LANES, tm, V7X_LANES), F32)],
        compiler_params=_params("parallel"),
        name="bf16_views",
    )(x2d))


def _layer(h, views, bsz, seq, alpha, layer, last, w_a, w_b, w_dt, b_gate, conv_w, conv_b, dt_bias, a_log, d_skip,
           ssm_norm_g, w_up_ssd, w_up_sb, w_up_dil, w_out, ln1_g, ln1_b, router_w, router_bias,
           w_gate_e, w_up_e, w_down_e, ln2_g, ln2_b):
    t = bsz * seq
    h_b = views[0]
    dil_w = 3 * DIL_WIDTH
    w_dt = w_dt.reshape(D_MODEL, SSD_GROUPS, SSD_HEADS_PER_GROUP)
    w_dt = jnp.pad(w_dt, ((0, 0), (0, 0), (0, DT_LANES - SSD_HEADS_PER_GROUP))).reshape(D_MODEL, DT_WIDTH)
    proj_a = _matmul(h_b, w_a, BF16, MM_TM, MM_TN, layer=layer).reshape(bsz, seq, N_PROJ_A)
    proj2d = _matmul(h_b, w_b, BF16, MM_TM, MM_TN, layer=layer,
                     col_ranges=((WB_SB, WB_DIL + dil_w), (WB_GATE, WB_END)))
    dt_raw = _matmul(h_b, w_dt.astype(BF16), F32, MM_TM, DT_WIDTH)
    proj = proj2d.reshape(bsz, seq, N_PROJ_B)

    y_ssd = _ssd(proj_a, dt_raw.reshape(bsz, seq, DT_WIDTH), conv_w, conv_b, dt_bias, a_log, d_skip, ssm_norm_g)
    y_sb = _stick_breaking(proj)

    assert DIL_PAIRS[0][1] == 1 and all(w // d == DIL_BLOCK for w, d in DIL_PAIRS)
    o, lse = _band_attention(proj, PB_DIL)
    dil = [o.reshape(t, DIL_WIDTH), lse.reshape(t, V7X_LANES)]
    for g, (_, r) in enumerate(DIL_PAIRS[1:], start=1):
        proj_g = _matmul(views[g].reshape(t, D_MODEL), w_b, BF16, MM_TM, MM_TN, layer=layer,
                         col_ranges=((WB_DIL + g * dil_w, WB_DIL + (g + 1) * dil_w),))
        o, lse = _band_attention(proj_g.reshape(bsz * r, seq // r, dil_w), 0)
        dil += [o.reshape(bsz, r, seq // r, DIL_WIDTH), lse.reshape(bsz, r, seq // r, V7X_LANES)]
    y_dil = _dilated_mixture(*dil, seq)

    merged = _merge_up(y_ssd.reshape(t, SSD_INNER), y_sb.reshape(t, SB_WIDTH), y_dil, layer,
                       w_up_ssd, w_up_sb, w_up_dil, proj2d, b_gate.astype(F32).reshape(N_BRANCH, D_MODEL))
    h1, h1_p, meta, wts, counts = _out_ln(merged, layer, w_out, h, ln1_g, ln1_b, router_w, router_bias, alpha)

    pos, tile_expert, n_used, fill = _route_plan(meta, counts[:, 0].astype(jnp.int32), MOE_TM)
    y_p = _moe_experts(h1_p, pos, tile_expert, n_used, fill, layer, w_gate_e, w_up_e, w_down_e)
    return _combine_ln(h1, y_p, pos, wts.T, ln2_g, ln2_b, alpha, bsz, seq, emit_views=not last)


def kernel(x, w_in, b_gate, conv_w, conv_b, dt_bias, a_log, d_skip, ssm_norm_g, w_up_ssd, w_up_sb, w_up_dil,
           w_out, ln1_g, ln1_b, router_w, router_bias, w_gate_e, w_up_e, w_down_e, ln2_g, ln2_b):
    bsz, seq, d = x.shape
    depth = w_in.shape[0]
    alpha = (2 * depth) ** 0.25
    h = x.reshape(bsz * seq, d)
    views = _bf16_views(h, bsz, seq)
    assert w_in.shape[2] == _IN_END
    w_a = _cast_cols(w_in, 0, N_PROJ_A)
    w_b = _cast_cols(w_in, _IN_DT1, WB_END)
    w_up_ssd, w_up_sb, w_up_dil, w_out, w_gate_e, w_up_e, w_down_e = (
        w.astype(BF16) for w in (w_up_ssd, w_up_sb, w_up_dil, w_out, w_gate_e, w_up_e, w_down_e))
    for l in range(depth):
        h, views = _layer(h, views, bsz, seq, alpha, l, l == depth - 1, w_a, w_b, w_in[l, :, _IN_DT0:_IN_DT1],
                          b_gate[l], conv_w[l], conv_b[l], dt_bias[l], a_log[l], d_skip[l], ssm_norm_g[l],
                          w_up_ssd, w_up_sb, w_up_dil, w_out, ln1_g[l], ln1_b[l], router_w, router_bias,
                          w_gate_e, w_up_e, w_down_e, ln2_g[l], ln2_b[l])
    return h.reshape(bsz, seq, d)
```

```python
import functools

import jax
import jax.numpy as jnp
from jax import lax
from jax.experimental import pallas as pl
from jax.experimental.pallas import tpu as pltpu

F32 = jnp.float32
BF16 = jnp.bfloat16

D_MODEL = 2048
SSD_HEADS = 32
SSD_HEAD_DIM = 64
SSD_INNER = SSD_HEADS * SSD_HEAD_DIM
SSD_STATE = 128
SSD_GROUPS = 4
SSD_HEADS_PER_GROUP = SSD_HEADS // SSD_GROUPS
SSD_GROUP_WIDTH = SSD_HEADS_PER_GROUP * SSD_HEAD_DIM
SSD_CONV = 4
SSD_CHUNK = 256
SSD_BC = SSD_GROUPS * SSD_STATE
SB_HEADS = 16
SB_HEAD_DIM = 128
SB_WIDTH = SB_HEADS * SB_HEAD_DIM
DIL_PAIRS = ((128, 1), (512, 4), (2048, 16))
DIL_HEADS = 8
DIL_HEAD_DIM = 128
DIL_WIDTH = DIL_HEADS * DIL_HEAD_DIM
DIL_BLOCK = 128
N_BRANCH = 3
N_EXPERTS = 16
EXPERTS_PER_GROUP = 4
N_EXPERT_GROUPS = N_EXPERTS // EXPERTS_PER_GROUP
D_EXPERT = 1024
EPS = 1e-5
NEG = -1e30
LOG2E = 1.4426950408889634
F32_SUBNORMAL_EXP2 = -128.0

_IN_DT0 = SSD_INNER + SSD_INNER + 2 * SSD_BC
_IN_DT1 = _IN_DT0 + SSD_HEADS
_IN_END = _IN_DT1 + 3 * SB_WIDTH + 3 * len(DIL_PAIRS) * DIL_WIDTH + N_BRANCH * D_MODEL
COL_Z = 0
COL_X = SSD_INNER
COL_B = COL_X + SSD_INNER
COL_C = COL_B + SSD_BC
N_PROJ_A = COL_C + SSD_BC
WB_SB = 0
WB_DIL = WB_SB + 3 * SB_WIDTH
WB_GATE = WB_DIL + 3 * len(DIL_PAIRS) * DIL_WIDTH
WB_END = WB_GATE + N_BRANCH * D_MODEL
PB_SB = 0
PB_DIL = PB_SB + 3 * SB_WIDTH
PB_GATE = PB_DIL + 3 * DIL_WIDTH
N_PROJ_B = PB_GATE + N_BRANCH * D_MODEL
DT_LANES = 128
DT_WIDTH = SSD_GROUPS * DT_LANES

V7X_LANES = 128
V7X_VMEM_LIMIT = 48 * 1024 * 1024
MM_TM, MM_TN = 1024, 1024
CAST_TK = 1024
CAST_SPILL = 32
MERGE_TM, MERGE_TN = 512, 512
ROW_TM = 256
SB_BLOCK = 256
SB_HEADS_PER_STEP = 4
SB_EXP2_CLAMP = 64.0
MOE_TM = 256


def _params(*sem):
    return pltpu.CompilerParams(dimension_semantics=sem, vmem_limit_bytes=V7X_VMEM_LIMIT)


def _dot(a, b):
    return jnp.dot(a, b, preferred_element_type=F32)


def _dot_nt(a, b):
    return lax.dot_general(a, b, (((1,), (1,)), ((), ())), preferred_element_type=F32)


def _split_dot(x, m, passes):
    acc = None
    r = x
    for p in range(passes):
        s = r.astype(BF16)
        d = _dot(s, m)
        acc = d if acc is None else acc + d
        if p + 1 < passes:
            r = r - s.astype(F32)
    return acc


def _split_dot_left(m, x, passes):
    acc = None
    r = x
    for p in range(passes):
        s = r.astype(BF16)
        d = _dot(m, s)
        acc = d if acc is None else acc + d
        if p + 1 < passes:
            r = r - s.astype(F32)
    return acc


def _softplus(x):
    return jnp.maximum(x, 0.0) + jnp.log1p(jnp.exp(-jnp.abs(x)))


def _silu(x):
    return x * jax.nn.sigmoid(x)


def _layer_norm(y, g, b):
    mu = jnp.mean(y, axis=-1, keepdims=True)
    yc = y - mu
    var = jnp.mean(yc * yc, axis=-1, keepdims=True)
    return yc * lax.rsqrt(var + EPS) * g + b


def _pack_halves(x):
    n = x.shape[1] // 2
    bits = pltpu.bitcast(x.astype(BF16).astype(F32), jnp.uint32)
    return (bits[:, n:] & jnp.uint32(0xFFFF0000)) | (bits[:, :n] >> 16)


def _unpack_halves(p):
    lo = pltpu.bitcast(p << 16, F32)
    hi = pltpu.bitcast(p & jnp.uint32(0xFFFF0000), F32)
    return jnp.concatenate([lo, hi], axis=1)


def _start_row_gather(idx_ref, base, src_hbm, dst, sem):
    def issue(r, carry):
        pltpu.make_async_copy(src_hbm.at[pl.ds(idx_ref[base + r], 1)], dst.at[pl.ds(r, 1)], sem).start()
        return carry

    lax.fori_loop(0, dst.shape[0], issue, 0, unroll=8)


def _wait_row_gather(src_hbm, dst, sem):
    pltpu.make_async_copy(src_hbm.at[pl.ds(0, dst.shape[0])], dst, sem).wait()


def _expander(rows, cols, width):
    h = lax.broadcasted_iota(jnp.int32, (rows, cols), 0)
    j = lax.broadcasted_iota(jnp.int32, (rows, cols), 1)
    return jnp.where(h == j // width, 1.0, 0.0).astype(BF16)


def _mm_kernel(a_ref, w_ref, o_ref):
    o_ref[...] = _dot(a_ref[...], w_ref[...]).astype(o_ref.dtype)


def _matmul(a, w, out_dtype, tm, tn, col_ranges=None, layer=None):
    m, k = a.shape
    tm, tn = min(tm, m), min(tn, w.shape[-1])
    col_ranges = col_ranges or ((0, w.shape[-1]),)
    assert all(c0 % tn == 0 and c1 % tn == 0 for c0, c1 in col_ranges)
    n = sum(c1 - c0 for c0, c1 in col_ranges)

    def w_block(j):
        blk, first = None, 0
        for c0, c1 in col_ranges:
            here = c0 // tn + (j - first)
            blk = here if blk is None else jnp.where(j >= first, here, blk)
            first += (c1 - c0) // tn
        return blk

    return pl.pallas_call(
        _mm_kernel,
        out_shape=jax.ShapeDtypeStruct((m, n), out_dtype),
        grid=(m // tm, n // tn),
        in_specs=[pl.BlockSpec((tm, k), lambda i, j: (i, 0)),
                  pl.BlockSpec((k, tn), lambda i, j: (0, w_block(j))) if layer is None else
                  pl.BlockSpec((None, k, tn), lambda i, j: (layer, 0, w_block(j)))],
        out_specs=pl.BlockSpec((tm, tn), lambda i, j: (i, j)),
        compiler_params=_params("parallel", "parallel"),
        name="matmul",
    )(a, w)


def _cast_cols_kernel(a_ref, b_ref, o_ref, *, shift):
    a = a_ref[0]
    if shift:
        a = jnp.concatenate([a, b_ref[0]], axis=0)[shift:shift + a.shape[0], :]
    o_ref[0] = a.T.astype(o_ref.dtype)


def _cast_cols(w_t, col0, n_out):
    depth, n_in, k = w_t.shape
    tk = CAST_TK
    tn = min(MM_TN, n_out)
    shift = col0 % tn
    base = col0 - shift
    assert shift <= CAST_SPILL and (base + tn) % CAST_SPILL == 0 and tn % CAST_SPILL == 0
    assert n_out % tn == 0 and k % tk == 0 and col0 + n_out <= n_in
    return pl.pallas_call(
        functools.partial(_cast_cols_kernel, shift=shift),
        out_shape=jax.ShapeDtypeStruct((depth, k, n_out), BF16),
        grid=(depth, k // tk, n_out // tn),
        in_specs=[pl.BlockSpec((1, tn, tk), lambda l, i, j: (l, base // tn + j, i)),
                  pl.BlockSpec((1, CAST_SPILL, tk), lambda l, i, j: (l, (base + tn * (j + 1)) // CAST_SPILL, i))],
        out_specs=pl.BlockSpec((1, tk, tn), lambda l, i, j: (l, i, j)),
        compiler_params=_params("parallel", "parallel", "parallel"),
        name="cast_cols",
    )(w_t, w_t)


def _ssd_kernel(z_ref, x_ref, b_ref, c_ref, dt_ref, cwx_ref, cwb_ref, cwc_ref, cbx_ref, cbb_ref, cbc_ref,
                dtb_ref, alog_ref, dskip_ref, g_ref, o_ref,
                state_ref, xpad_ref, bpad_ref, cpad_ref, xs_ref, bs_ref, cs_ref, y_ref,
                dts_ref, acs_ref, ecs_ref, dte_ref):
    chunk = x_ref.shape[1]
    pad = 8

    @pl.when(pl.program_id(1) == 0)
    def _():
        state_ref[...] = jnp.zeros_like(state_ref)
        xpad_ref[0:pad, :] = jnp.zeros((pad, xpad_ref.shape[1]), F32)
        bpad_ref[0:pad, :] = jnp.zeros((pad, bpad_ref.shape[1]), F32)
        cpad_ref[0:pad, :] = jnp.zeros((pad, cpad_ref.shape[1]), F32)

    def conv_silu(src_ref, pad_ref, cw_ref, cb_ref, dst_ref):
        pad_ref[pad:pad + chunk, :] = src_ref[0].astype(F32)
        acc = cb_ref[...]
        for k in range(SSD_CONV):
            off = pad - (SSD_CONV - 1) + k
            acc = acc + cw_ref[k:k + 1, :] * pad_ref[off:off + chunk, :]
        dst_ref[...] = _silu(acc)
        pad_ref[0:pad, :] = pad_ref[chunk:chunk + pad, :]

    conv_silu(x_ref, xpad_ref, cwx_ref, cbx_ref, xs_ref)
    conv_silu(b_ref, bpad_ref, cwb_ref, cbb_ref, bs_ref)
    conv_silu(c_ref, cpad_ref, cwc_ref, cbc_ref, cs_ref)

    dt = _softplus(dt_ref[0] + dtb_ref[...])
    a_dt = dt * (-jnp.exp(alog_ref[...]))
    row = lax.broadcasted_iota(jnp.int32, (chunk, chunk), 0)
    col = lax.broadcasted_iota(jnp.int32, (chunk, chunk), 1)
    causal = col <= row
    tri = jnp.where(causal, 1.0, 0.0).astype(BF16)
    a_cs = _split_dot_left(tri, a_dt, 3)
    last = a_cs[chunk - 1:chunk, :]
    dts_ref[...] = dt
    acs_ref[...] = a_cs
    ecs_ref[...] = jnp.exp(a_cs)
    dte_ref[...] = jnp.exp(last - a_cs)

    expand = _expander(DT_LANES, SSD_GROUP_WIDTH, SSD_HEAD_DIM)
    lane = lax.broadcasted_iota(jnp.int32, (chunk, V7X_LANES), 1)
    heads_per_tile = V7X_LANES // SSD_HEAD_DIM

    def group(g, carry):
        ch0 = pl.multiple_of(g * SSD_GROUP_WIDTH, SSD_GROUP_WIDTH)
        st0 = pl.multiple_of(g * SSD_STATE, SSD_STATE)
        dt0 = pl.multiple_of(g * DT_LANES, DT_LANES)
        xg = xs_ref[:, pl.ds(ch0, SSD_GROUP_WIDTH)]
        bg = bs_ref[:, pl.ds(st0, SSD_STATE)]
        cg = cs_ref[:, pl.ds(st0, SSD_STATE)].astype(BF16)
        csg = acs_ref[:, pl.ds(dt0, DT_LANES)]
        dt_x = _split_dot(dts_ref[:, pl.ds(dt0, DT_LANES)], expand, 2)
        ecs_x = _split_dot(ecs_ref[:, pl.ds(dt0, DT_LANES)], expand, 2)
        dte_x = _split_dot(dte_ref[:, pl.ds(dt0, DT_LANES)], expand, 2)
        xdt = xg * dt_x
        xdt_b = xdt.astype(BF16)
        cb = _dot_nt(cg, bg.astype(BF16))
        cs_t = csg.T
        prev = state_ref[g]
        y_off = _dot(cg, prev.astype(BF16)) * ecs_x
        for t in range(SSD_GROUP_WIDTH // V7X_LANES):
            lanes = slice(t * V7X_LANES, (t + 1) * V7X_LANES)
            xdt_t = xdt_b[:, lanes]
            y_t = None
            for u in range(heads_per_tile):
                r = t * heads_per_tile + u
                seg = csg[:, r:r + 1] - cs_t[r:r + 1, :]
                m = (cb * jnp.exp(jnp.where(causal, seg, NEG))).astype(BF16)
                y_r = _dot(m, xdt_t)
                y_t = y_r if y_t is None else jnp.where(lane // SSD_HEAD_DIM == u, y_r, y_t)
            c0 = pl.multiple_of(ch0 + t * V7X_LANES, V7X_LANES)
            y_ref[:, pl.ds(c0, V7X_LANES)] = (y_t + y_off[:, lanes]
                                              + dskip_ref[:, pl.ds(c0, V7X_LANES)] * xg[:, lanes])
        new = ecs_x[chunk - 1:chunk, :] * prev + _dot(bg.T.astype(BF16), (xdt * dte_x).astype(BF16))
        state_ref[g] = new
        return carry

    lax.fori_loop(0, SSD_GROUPS, group, 0)

    yf = y_ref[...] * _silu(z_ref[0].astype(F32))
    ms = jnp.mean(yf * yf, axis=-1, keepdims=True)
    o_ref[0] = (yf * lax.rsqrt(ms + EPS) * g_ref[...]).astype(o_ref.dtype)


def _ssd(proj, dt_raw, conv_w, conv_b, dt_bias, a_log, d_skip, norm_g):
    bsz, seq, _ = proj.shape
    chunk = min(SSD_CHUNK, seq)
    nc = seq // chunk

    def spread_heads(v):
        v = v.astype(F32).reshape(SSD_GROUPS, SSD_HEADS_PER_GROUP)
        return jnp.pad(v, ((0, 0), (0, DT_LANES - SSD_HEADS_PER_GROUP))).reshape(1, DT_WIDTH)

    cw = conv_w.astype(F32)
    cbias = conv_b.astype(F32).reshape(1, -1)
    x1, b1 = SSD_INNER, SSD_INNER + SSD_BC
    consts = [cw[:, :x1], cw[:, x1:b1], cw[:, b1:], cbias[:, :x1], cbias[:, x1:b1], cbias[:, b1:],
              spread_heads(dt_bias), spread_heads(a_log),
              jnp.repeat(d_skip.astype(F32), SSD_HEAD_DIM).reshape(1, SSD_INNER),
              norm_g.astype(F32).reshape(1, SSD_INNER)]

    def col_block(width, col):
        return pl.BlockSpec((1, chunk, width), lambda b, c: (b, c, col // width))

    def whole(a):
        return pl.BlockSpec(a.shape, lambda b, c: (0, 0))

    return pl.pallas_call(
        _ssd_kernel,
        out_shape=jax.ShapeDtypeStruct((bsz, seq, SSD_INNER), BF16),
        grid=(bsz, nc),
        in_specs=[col_block(SSD_INNER, COL_Z), col_block(SSD_INNER, COL_X),
                  col_block(SSD_BC, COL_B), col_block(SSD_BC, COL_C),
                  pl.BlockSpec((1, chunk, DT_WIDTH), lambda b, c: (b, c, 0))] + [whole(a) for a in consts],
        out_specs=pl.BlockSpec((1, chunk, SSD_INNER), lambda b, c: (b, c, 0)),
        scratch_shapes=[pltpu.VMEM((SSD_GROUPS, SSD_STATE, SSD_GROUP_WIDTH), F32),
                        pltpu.VMEM((chunk + 8, SSD_INNER), F32),
                        pltpu.VMEM((chunk + 8, SSD_BC), F32),
                        pltpu.VMEM((chunk + 8, SSD_BC), F32),
                        pltpu.VMEM((chunk, SSD_INNER), F32),
                        pltpu.VMEM((chunk, SSD_BC), F32),
                        pltpu.VMEM((chunk, SSD_BC), F32),
                        pltpu.VMEM((chunk, SSD_INNER), F32),
                        pltpu.VMEM((chunk, DT_WIDTH), F32),
                        pltpu.VMEM((chunk, DT_WIDTH), F32),
                        pltpu.VMEM((chunk, DT_WIDTH), F32),
                        pltpu.VMEM((chunk, DT_WIDTH), F32)],
        compiler_params=_params("parallel", "arbitrary"),
        name="ssd",
    )(proj, proj, proj, proj, dt_raw, *consts)


def _sb_kernel(q_ref, k_ref, v_ref, o_ref, *, blk, scale2):
    qi = pl.program_id(2)
    hd = SB_HEAD_DIM
    row = lax.broadcasted_iota(jnp.int32, (blk, blk), 0)
    col = lax.broadcasted_iota(jnp.int32, (blk, blk), 1)
    strict = col < row
    upper = jnp.where(row >= col, 1.0, 0.0).astype(BF16)
    upper2 = jnp.concatenate([upper, upper], axis=0)

    lanes = [slice(h * hd, (h + 1) * hd) for h in range(q_ref.shape[2] // hd)]

    def blocks(kb, rs, diagonal):
        start = pl.multiple_of(kb * blk, blk)
        zs = [_dot_nt(q_ref[0, :, sl], k_ref[0, pl.ds(start, blk), sl]) * scale2 for sl in lanes]
        sufs = []
        for z in zs:
            sp = jnp.maximum(z, jnp.log2(1.0 + jnp.exp2(jnp.minimum(z, SB_EXP2_CLAMP))))
            if diagonal:
                sp = jnp.where(strict, sp, 0.0)
            hi = pltpu.bitcast(pltpu.bitcast(sp, jnp.uint32) & jnp.uint32(0xFFFF0000), F32)
            parts = jnp.concatenate([hi.astype(BF16), (sp - hi).astype(BF16)], axis=1)
            sufs.append(_dot(parts, upper2))
        new_rs = tuple(r - suf[:, 0:1] for r, suf in zip(rs, sufs))
        pvs = []
        for z, suf, r, sl in zip(zs, sufs, rs, lanes):
            w = jnp.exp2(z - suf + r)
            if diagonal:
                w = jnp.where(strict, w, 0.0)
            pvs.append(_dot(w.astype(BF16), v_ref[0, pl.ds(start, blk), sl]))
        return new_rs, pvs

    def largest(rs):
        m = rs[0]
        for r in rs[1:]:
            m = jnp.maximum(m, r)
        return jnp.max(m)

    rs, accs = blocks(qi, tuple(jnp.zeros((blk, 1), F32) for _ in lanes), True)

    def more(c):
        return (c[0] >= 0) & (c[1] > F32_SUBNORMAL_EXP2)

    def body(c):
        kb, _, rs, accs = c
        rs, pvs = blocks(kb, rs, False)
        return kb - 1, largest(rs), rs, tuple(a + pv for a, pv in zip(accs, pvs))

    _, _, _, accs = lax.while_loop(more, body, (qi - 1, largest(rs), rs, tuple(accs)))
    o_ref[0] = jnp.concatenate(accs, axis=1).astype(o_ref.dtype)


def _stick_breaking(proj):
    bsz, seq, _ = proj.shape
    blk = min(SB_BLOCK, seq)
    qc, kc, vc = (PB_SB + i * SB_WIDTH for i in range(3))
    w = SB_HEADS_PER_STEP * SB_HEAD_DIM
    return pl.pallas_call(
        functools.partial(_sb_kernel, blk=blk, scale2=SB_HEAD_DIM ** -0.5 * LOG2E),
        out_shape=jax.ShapeDtypeStruct((bsz, seq, SB_WIDTH), BF16),
        grid=(bsz, SB_WIDTH // w, seq // blk),
        in_specs=[pl.BlockSpec((1, blk, w), lambda b, h, i: (b, i, qc // w + h)),
                  pl.BlockSpec((1, seq, w), lambda b, h, i: (b, 0, kc // w + h)),
                  pl.BlockSpec((1, seq, w), lambda b, h, i: (b, 0, vc // w + h))],
        out_specs=pl.BlockSpec((1, blk, w), lambda b, h, i: (b, i, h)),
        compiler_params=_params("parallel", "parallel", "arbitrary"),
        name="stick_breaking",
    )(proj, proj, proj)


def _dil_kernel(q_ref, kp_ref, kc_ref, vp_ref, vc_ref, o_ref, lse_ref, *, scale):
    bq = DIL_BLOCK
    blk = pl.program_id(1)
    i = lax.broadcasted_iota(jnp.int32, (bq, 2 * bq), 0)
    m = lax.broadcasted_iota(jnp.int32, (bq, 2 * bq), 1)
    dist = i - m + bq
    first = jnp.where(blk > 0, 0, bq)
    valid = (dist >= 0) & (dist <= bq) & (m >= first)
    lane = lax.broadcasted_iota(jnp.int32, (bq, V7X_LANES), 1)
    lse_all = jnp.zeros((bq, V7X_LANES), F32)
    for h in range(DIL_HEADS):
        sl = slice(h * DIL_HEAD_DIM, (h + 1) * DIL_HEAD_DIM)
        k = jnp.concatenate([kp_ref[0, :, sl], kc_ref[0, :, sl]], axis=0)
        v = jnp.concatenate([vp_ref[0, :, sl], vc_ref[0, :, sl]], axis=0)
        s = jnp.where(valid, _dot_nt(q_ref[0, :, sl], k) * scale, NEG)
        mx = jnp.max(s, axis=-1, keepdims=True)
        p = jnp.exp(s - mx)
        den = jnp.sum(p, axis=-1, keepdims=True)
        o_ref[0, :, sl] = (_dot(p.astype(BF16), v) / den).astype(o_ref.dtype)
        lse_all = jnp.where(lane == h, mx + jnp.log(den), lse_all)
    lse_ref[0] = lse_all


def _band_attention(qkv, col):
    nz, lc, _ = qkv.shape
    bq = DIL_BLOCK
    w = DIL_WIDTH
    assert lc % bq == 0 and col % w == 0

    def cur(part):
        return pl.BlockSpec((1, bq, w), lambda z, i: (z, i, col // w + part))

    def prev(part):
        return pl.BlockSpec((1, bq, w), lambda z, i: (z, jnp.maximum(i - 1, 0), col // w + part))

    return pl.pallas_call(
        functools.partial(_dil_kernel, scale=DIL_HEAD_DIM ** -0.5),
        out_shape=(jax.ShapeDtypeStruct((nz, lc, w), BF16),
                   jax.ShapeDtypeStruct((nz, lc, V7X_LANES), F32)),
        grid=(nz, lc // bq),
        in_specs=[cur(0), prev(1), cur(1), prev(2), cur(2)],
        out_specs=(pl.BlockSpec((1, bq, w), lambda z, i: (z, i, 0)),
                   pl.BlockSpec((1, bq, V7X_LANES), lambda z, i: (z, i, 0))),
        compiler_params=_params("parallel", "arbitrary"),
        name="dilated_attention",
    )(qkv, qkv, qkv, qkv, qkv)


def _residue_block(tm, seq, dilation, width):
    tiles = seq // tm
    return pl.BlockSpec((1, dilation, tm // dilation, width), lambda i, *_: (i // tiles, 0, i % tiles, 0))


def _emit_by_residue(x, strip_ref, dst_refs):
    tm, w = x.shape
    strips = w // V7X_LANES
    for s in range(strips):
        strip_ref[s] = x[:, s * V7X_LANES:(s + 1) * V7X_LANES]
    for dst in dst_refs:
        d, n = dst.shape[1], dst.shape[2]
        for c in range(d):
            for s in range(strips):
                dst[0, c, :, s * V7X_LANES:(s + 1) * V7X_LANES] = strip_ref[s, pl.ds(c, n, stride=d), :].astype(dst.dtype)


def _dilmix_kernel(o0_ref, l0_ref, o1_ref, l1_ref, o2_ref, l2_ref, out_ref, strip_ref, lse_ref):
    tm = out_ref.shape[0]
    heads = DIL_WIDTH // DIL_HEAD_DIM
    for gi, (o_ref, l_ref) in enumerate(((o1_ref, l1_ref), (o2_ref, l2_ref))):
        d = o_ref.shape[1]
        n = tm // d
        for c in range(d):
            lse_ref[gi, pl.ds(c, n, stride=d), :] = l_ref[0, c]
            for j in range(heads):
                strip_ref[gi, j, pl.ds(c, n, stride=d), :] = (
                    o_ref[0, c, :, j * DIL_HEAD_DIM:(j + 1) * DIL_HEAD_DIM].astype(F32))
    ls = [l0_ref[...], lse_ref[0], lse_ref[1]]
    mx = jnp.maximum(jnp.maximum(ls[0], ls[1]), ls[2])
    es = [jnp.exp(l - mx) for l in ls]
    inv = 1.0 / (es[0] + es[1] + es[2])
    ws = [e * inv for e in es]
    for j in range(heads):
        sl = slice(j * DIL_HEAD_DIM, (j + 1) * DIL_HEAD_DIM)
        acc = (ws[0][:, j:j + 1] * o0_ref[:, sl].astype(F32) + ws[1][:, j:j + 1] * strip_ref[0, j]
               + ws[2][:, j:j + 1] * strip_ref[1, j])
        out_ref[:, sl] = acc.astype(out_ref.dtype)


def _dilated_mixture(o0, l0, o1, l1, o2, l2, seq):
    t = o0.shape[0]
    tm = min(ROW_TM, t)
    heads = DIL_WIDTH // DIL_HEAD_DIM
    assert DIL_HEAD_DIM == V7X_LANES
    o_spec = pl.BlockSpec((tm, DIL_WIDTH), lambda i: (i, 0))
    l_spec = pl.BlockSpec((tm, V7X_LANES), lambda i: (i, 0))
    res = [spec for o in (o1, o2) for spec in (_residue_block(tm, seq, o.shape[1], DIL_WIDTH),
                                               _residue_block(tm, seq, o.shape[1], V7X_LANES))]
    return pl.pallas_call(
        _dilmix_kernel,
        out_shape=jax.ShapeDtypeStruct((t, DIL_WIDTH), BF16),
        grid=(t // tm,),
        in_specs=[o_spec, l_spec] + res,
        out_specs=o_spec,
        scratch_shapes=[pltpu.VMEM((2, heads, tm, DIL_HEAD_DIM), F32), pltpu.VMEM((2, tm, V7X_LANES), F32)],
        compiler_params=_params("parallel"),
        name="dilated_mix",
    )(o0, l0, o1, l1, o2, l2)


def _merge_kernel(ya_ref, yb_ref, yc_ref, ua_ref, ub_ref, uc_ref, ga_ref, gb_ref, gc_ref, bg_ref, o_ref):
    acc = None
    for i, (y_ref, u_ref, g_ref) in enumerate(((ya_ref, ua_ref, ga_ref), (yb_ref, ub_ref, gb_ref),
                                               (yc_ref, uc_ref, gc_ref))):
        gate = jax.nn.sigmoid(g_ref[...].astype(F32) + bg_ref[i:i + 1, :])
        term = gate * _dot(y_ref[...], u_ref[...])
        acc = term if acc is None else acc + term
    o_ref[...] = acc.astype(o_ref.dtype)


def _merge_up(y_ssd, y_sb, y_dil, layer, u_ssd, u_sb, u_dil, proj2d, b_gate):
    t = y_ssd.shape[0]
    tm, tn = min(MERGE_TM, t), MERGE_TN

    def rows(a):
        return pl.BlockSpec((tm, a.shape[1]), lambda i, j: (i, 0))

    def cols(a):
        return pl.BlockSpec((None, a.shape[1], tn), lambda i, j: (layer, 0, j))

    def gate(k):
        return pl.BlockSpec((tm, tn), lambda i, j: (i, (PB_GATE + k * D_MODEL) // tn + j))

    return pl.pallas_call(
        _merge_kernel,
        out_shape=jax.ShapeDtypeStruct((t, D_MODEL), BF16),
        grid=(t // tm, D_MODEL // tn),
        in_specs=[rows(y_ssd), rows(y_sb), rows(y_dil), cols(u_ssd), cols(u_sb), cols(u_dil),
                  gate(0), gate(1), gate(2), pl.BlockSpec((N_BRANCH, tn), lambda i, j: (0, j))],
        out_specs=pl.BlockSpec((tm, tn), lambda i, j: (i, j)),
        compiler_params=_params("parallel", "parallel"),
        name="merge_up",
    )(y_ssd, y_sb, y_dil, u_ssd, u_sb, u_dil, proj2d, proj2d, proj2d, b_gate)


def _outln_kernel(m_ref, w_ref, h_ref, g_ref, b_ref, rwh_ref, rwl_ref, rb_ref, of_ref, op_ref, meta_ref, wts_ref,
                  cnt_ref, *, alpha):
    @pl.when(pl.program_id(0) == 0)
    def _():
        cnt_ref[...] = jnp.zeros_like(cnt_ref)

    y = alpha * h_ref[...] + _dot(m_ref[...], w_ref[...])
    out = _layer_norm(y, g_ref[...], b_ref[...])
    of_ref[...] = out
    op_ref[...] = _pack_halves(out)
    e1, e2, w1, w2 = _route(out, rwh_ref[...], rwl_ref[...], rb_ref[...])
    r1, r2 = _expert_ranks(e1, e2, cnt_ref)
    for row, v in enumerate((e1, e2, r1, r2)):
        meta_ref[row:row + 1, :] = v
    wts_ref[0:1, :] = w1
    wts_ref[1:2, :] = w2


def _out_ln(merged, layer, w_out, h, g, b, router_w, router_bias, alpha):
    t, d = h.shape
    tm = min(ROW_TM, t)
    row = pl.BlockSpec((tm, d), lambda i: (i, 0))
    vec = pl.BlockSpec((1, d), lambda i: (0, 0))
    rw = jnp.pad(router_w.astype(F32), ((0, 0), (0, V7X_LANES - N_EXPERTS)))
    rw_hi = rw.astype(BF16)
    rw_lo = (rw - rw_hi.astype(F32)).astype(BF16)
    rw_spec = pl.BlockSpec((d, V7X_LANES), lambda i: (0, 0))
    return pl.pallas_call(
        functools.partial(_outln_kernel, alpha=alpha),
        out_shape=(jax.ShapeDtypeStruct((t, d), F32), jax.ShapeDtypeStruct((t, d // 2), jnp.uint32),
                   jax.ShapeDtypeStruct((4, t), jnp.int32), jax.ShapeDtypeStruct((2, t), F32),
                   jax.ShapeDtypeStruct((N_EXPERTS, V7X_LANES), F32)),
        grid=(t // tm,),
        in_specs=[row, pl.BlockSpec((None, d, d), lambda i: (layer, 0, 0)), row, vec, vec,
                  rw_spec, rw_spec, pl.BlockSpec((N_EXPERTS, 1), lambda i: (0, 0))],
        out_specs=(row, pl.BlockSpec((tm, d // 2), lambda i: (i, 0)),
                   pl.BlockSpec((4, tm), lambda i: (0, i)), pl.BlockSpec((2, tm), lambda i: (0, i)),
                   pl.BlockSpec((N_EXPERTS, V7X_LANES), lambda i: (0, 0))),
        compiler_params=_params("arbitrary"),
        name="out_ln",
    )(merged, w_out, h, g.reshape(1, d), b.reshape(1, d),
      rw_hi, rw_lo, router_bias.astype(F32).reshape(N_EXPERTS, 1))


def _route(h, rw_hi, rw_lo, rb):
    h_hi = h.astype(BF16)
    h_lo = (h - h_hi.astype(F32)).astype(BF16)
    logits = _dot(h_hi, rw_hi) + (_dot(h_lo, rw_hi) + _dot(h_hi, rw_lo))
    logits = logits.T[0:N_EXPERTS, :]
    mx = jnp.max(logits, axis=0, keepdims=True)
    ex = jnp.exp(logits - mx)
    aff = ex / jnp.sum(ex, axis=0, keepdims=True)
    sel = aff + rb
    s = [sel[e:e + 1, :] for e in range(N_EXPERTS)]
    a = [aff[e:e + 1, :] for e in range(N_EXPERTS)]
    n = EXPERTS_PER_GROUP
    best_score, best = None, None
    for g in range(N_EXPERT_GROUPS):
        grp = s[g * n:(g + 1) * n]
        score = None
        for i in range(n):
            for j in range(i + 1, n):
                pair = grp[i] + grp[j]
                score = pair if score is None else jnp.maximum(score, pair)
        if best is None:
            best_score, best = score, jnp.zeros_like(score, dtype=jnp.int32)
        else:
            take = score > best_score
            best_score = jnp.where(take, score, best_score)
            best = jnp.where(take, g, best)
    sv, av = [], []
    for r in range(n):
        sr, ar = s[r], a[r]
        for g in range(1, N_EXPERT_GROUPS):
            sr = jnp.where(best == g, s[g * n + r], sr)
            ar = jnp.where(best == g, a[g * n + r], ar)
        sv.append(sr)
        av.append(ar)

    def arg_top(vals, skip):
        top_v, top_i, top_a = None, None, None
        for r in range(n):
            v = vals[r] if skip is None else jnp.where(skip == r, -jnp.inf, vals[r])
            if top_v is None:
                top_v, top_i, top_a = v, jnp.zeros_like(best), av[0]
            else:
                take = v > top_v
                top_v = jnp.where(take, v, top_v)
                top_i = jnp.where(take, r, top_i)
                top_a = jnp.where(take, av[r], top_a)
        return top_i, top_a

    i1, a1 = arg_top(sv, None)
    i2, a2 = arg_top(sv, i1)
    tot = a1 + a2
    return best * n + i1, best * n + i2, a1 / tot, a2 / tot


def _expert_ranks(e1, e2, count_ref):
    tm = e1.shape[1]
    expert = lax.broadcasted_iota(jnp.int32, (N_EXPERTS, tm), 0)
    hit1, hit2 = expert == e1, expert == e2
    both = jnp.where(hit1, 1.0, 0.0) + jnp.where(hit2, 1.0, 0.0)
    earlier = lax.broadcasted_iota(jnp.int32, (tm, tm), 0) < lax.broadcasted_iota(jnp.int32, (tm, tm), 1)
    before = _dot(both.astype(BF16), jnp.where(earlier, 1.0, 0.0).astype(BF16)) + count_ref[:, 0:1]
    r1 = jnp.sum(jnp.where(hit1, before, 0.0), axis=0, keepdims=True)
    r2 = jnp.sum(jnp.where(hit2, before, 0.0), axis=0, keepdims=True)
    count_ref[...] = count_ref[...] + jnp.sum(both, axis=1, keepdims=True)
    return r1.astype(jnp.int32), r2.astype(jnp.int32)


def _moe_kernel(te_ref, nu_ref, fill_ref, pos_ref, x_hbm, wg_ref, wu_ref, wd_ref, o_ref, xbuf, sem, src_ref):
    i = pl.program_id(0)
    n_used = nu_ref[0]
    tm = xbuf.shape[1]
    n_tokens = pos_ref.shape[0] // 2

    @pl.when(i == 0)
    def _():
        def pad_expert(e, carry):
            def pad_slot(s, c):
                src_ref[s] = 0
                return c

            return lax.fori_loop(fill_ref[e], fill_ref[N_EXPERTS + e], pad_slot, carry)

        lax.fori_loop(0, N_EXPERTS, pad_expert, 0)
        for k in range(2):
            def place(tok, carry):
                src_ref[pos_ref[k * n_tokens + tok]] = tok
                return carry

            lax.fori_loop(0, n_tokens, place, 0, unroll=8)

    def fetch(tile):
        slot = tile % 2
        _start_row_gather(src_ref, tile * tm, x_hbm, xbuf.at[slot], sem.at[slot])

    @pl.when(i == 0)
    def _():
        fetch(i)

    @pl.when(i + 1 < n_used)
    def _():
        fetch(i + 1)

    @pl.when(i < n_used)
    def _():
        slot = i % 2
        _wait_row_gather(x_hbm, xbuf.at[slot], sem.at[slot])
        x = _unpack_halves(xbuf[slot]).astype(BF16)
        act = _silu(_dot(x, wg_ref[...])) * _dot(x, wu_ref[...])
        o_ref[...] = _pack_halves(_dot(act.astype(BF16), wd_ref[...]))

    @pl.when(i >= n_used)
    def _():
        o_ref[...] = jnp.zeros_like(o_ref)


def _route_plan(meta, counts, tm):
    t = meta.shape[1]
    padded = (counts + tm - 1) // tm * tm
    ends = jnp.cumsum(padded)
    starts = ends - padded
    one_hot = meta[0:2, :, None] == jnp.arange(N_EXPERTS, dtype=jnp.int32)
    pos = jnp.sum(jnp.where(one_hot, starts, 0), axis=-1) + meta[2:4]
    n_tiles = (2 * t + N_EXPERTS * tm) // tm
    n_used = (ends[-1] // tm).astype(jnp.int32)
    tile_start = jnp.minimum(jnp.arange(n_tiles, dtype=jnp.int32), n_used - 1) * tm
    owner = jnp.sum((ends[None, :] <= tile_start[:, None]).astype(jnp.int32), axis=1)
    fill = jnp.concatenate([starts + counts, ends]).astype(jnp.int32)
    return pos.reshape(-1), jnp.minimum(owner, N_EXPERTS - 1), n_used.reshape(1), fill


def _moe_experts(x_packed, pos, tile_expert, n_used, fill, layer, w_gate, w_up, w_down):
    n_tiles = tile_expert.shape[0]
    dp = x_packed.shape[1]
    tm = MOE_TM
    d, de = w_gate.shape[2], w_gate.shape[3]

    def weights(i, te, *_):
        return (layer, te[i], 0, 0)

    return pl.pallas_call(
        _moe_kernel,
        out_shape=jax.ShapeDtypeStruct((n_tiles * tm, dp), jnp.uint32),
        grid_spec=pltpu.PrefetchScalarGridSpec(
            num_scalar_prefetch=4,
            grid=(n_tiles,),
            in_specs=[pl.BlockSpec(memory_space=pl.ANY),
                      pl.BlockSpec((None, None, d, de), weights),
                      pl.BlockSpec((None, None, d, de), weights),
                      pl.BlockSpec((None, None, de, d), weights)],
            out_specs=pl.BlockSpec((tm, dp), lambda i, *_: (i, 0)),
            scratch_shapes=[pltpu.VMEM((2, tm, dp), jnp.uint32), pltpu.SemaphoreType.DMA((2,)),
                            pltpu.SMEM((n_tiles * tm,), jnp.int32)]),
        compiler_params=_params("arbitrary"),
        name="moe_experts",
    )(tile_expert, n_used, fill, pos, x_packed, w_gate, w_up, w_down)


def _ln2_kernel(pos_ref, h_ref, y_hbm, w_ref, g_ref, b_ref, of_ref, *rest, alpha, n_views):
    views, (rbuf, sem), strips = rest[:n_views], rest[n_views:n_views + 2], rest[n_views + 2:]
    i = pl.program_id(0)
    n_tiles = pl.num_programs(0)
    tm = h_ref.shape[0]

    def fetch(tile):
        slot = tile % 2
        for k in range(2):
            _start_row_gather(pos_ref, (k * n_tiles + tile) * tm, y_hbm, rbuf.at[slot, k], sem.at[slot, k])

    @pl.when(i == 0)
    def _():
        fetch(i)

    @pl.when(i + 1 < n_tiles)
    def _():
        fetch(i + 1)

    slot = i % 2
    y = alpha * h_ref[...]
    w = w_ref[...]
    for k in range(2):
        _wait_row_gather(y_hbm, rbuf.at[slot, k], sem.at[slot, k])
        y = y + w[:, k:k + 1] * _unpack_halves(rbuf[slot, k])
    out = _layer_norm(y, g_ref[...], b_ref[...])
    of_ref[...] = out
    if views:
        views[0][...] = out.astype(BF16)
        _emit_by_residue(out, strips[0], views[1:])


def _view_shapes_specs(bsz, seq, d, tm):
    shapes = [jax.ShapeDtypeStruct((bsz * seq, d), BF16)]
    specs = [pl.BlockSpec((tm, d), lambda i, *_: (i, 0))]
    for _, r in DIL_PAIRS[1:]:
        shapes.append(jax.ShapeDtypeStruct((bsz, r, seq // r, d), BF16))
        specs.append(_residue_block(tm, seq, r, d))
    return shapes, specs


def _combine_ln(h, y_packed, pos_by_choice, wts, g, b, alpha, bsz, seq, emit_views):
    t, d = h.shape
    dp = y_packed.shape[1]
    tm = min(ROW_TM, t)
    row = pl.BlockSpec((tm, d), lambda i, pos: (i, 0))
    vec = pl.BlockSpec((1, d), lambda i, pos: (0, 0))
    v_shapes, v_specs = _view_shapes_specs(bsz, seq, d, tm) if emit_views else ([], [])
    scratch = [pltpu.VMEM((2, 2, tm, dp), jnp.uint32), pltpu.SemaphoreType.DMA((2, 2))]
    if emit_views:
        scratch.append(pltpu.VMEM((d // V7X_LANES, tm, V7X_LANES), F32))
    out = pl.pallas_call(
        functools.partial(_ln2_kernel, alpha=alpha, n_views=len(v_shapes)),
        out_shape=[jax.ShapeDtypeStruct((t, d), F32)] + v_shapes,
        grid_spec=pltpu.PrefetchScalarGridSpec(
            num_scalar_prefetch=1,
            grid=(t // tm,),
            in_specs=[row, pl.BlockSpec(memory_space=pl.ANY), pl.BlockSpec((tm, 2), lambda i, pos: (i, 0)), vec, vec],
            out_specs=[row] + v_specs,
            scratch_shapes=scratch),
        compiler_params=_params("arbitrary"),
        name="combine_ln",
    )(pos_by_choice, h, y_packed, wts, g.reshape(1, d), b.reshape(1, d))
    return out[0], tuple(out[1:])


def _views_kernel(x_ref, *rest):
    views, strip_ref = rest[:-1], rest[-1]
    x = x_ref[...]
    views[0][...] = x.astype(BF16)
    _emit_by_residue(x, strip_ref, views[1:])


def _bf16_views(x2d, bsz, seq):
    t, d = x2d.shape
    tm = min(ROW_TM, t)
    v_shapes, v_specs = _view_shapes_specs(bsz, seq, d, tm)
    return tuple(pl.pallas_call(
        _views_kernel,
        out_shape=v_shapes,
        grid=(t // tm,),
        in_specs=[pl.BlockSpec((tm, d), lambda i: (i, 0))],
        out_specs=v_specs,
        scratch_shapes=[pltpu.VMEM((d // V7X_LANES, tm, V7X_LANES), F32)],
        compiler_params=_params("parallel"),
        name="bf16_views",
    )(x2d))


def _layer(h, views, bsz, seq, alpha, layer, last, w_a, w_b, w_dt, b_gate, conv_w, conv_b, dt_bias, a_log, d_skip,
           ssm_norm_g, w_up_ssd, w_up_sb, w_up_dil, w_out, ln1_g, ln1_b, router_w, router_bias,
           w_gate_e, w_up_e, w_down_e, ln2_g, ln2_b):
    t = bsz * seq
    h_b = views[0]
    dil_w = 3 * DIL_WIDTH
    w_dt = w_dt.reshape(D_MODEL, SSD_GROUPS, SSD_HEADS_PER_GROUP)
    w_dt = jnp.pad(w_dt, ((0, 0), (0, 0), (0, DT_LANES - SSD_HEADS_PER_GROUP))).reshape(D_MODEL, DT_WIDTH)
    proj_a = _matmul(h_b, w_a, BF16, MM_TM, MM_TN, layer=layer).reshape(bsz, seq, N_PROJ_A)
    proj2d = _matmul(h_b, w_b, BF16, MM_TM, MM_TN, layer=layer,
                     col_ranges=((WB_SB, WB_DIL + dil_w), (WB_GATE, WB_END)))
    dt_raw = _matmul(h_b, w_dt, F32, MM_TM, DT_WIDTH)
    proj = proj2d.reshape(bsz, seq, N_PROJ_B)

    y_ssd = _ssd(proj_a, dt_raw.reshape(bsz, seq, DT_WIDTH), conv_w, conv_b, dt_bias, a_log, d_skip, ssm_norm_g)
    y_sb = _stick_breaking(proj)

    assert DIL_PAIRS[0][1] == 1 and all(w // d == DIL_BLOCK for w, d in DIL_PAIRS)
    o, lse = _band_attention(proj, PB_DIL)
    dil = [o.reshape(t, DIL_WIDTH), lse.reshape(t, V7X_LANES)]
    for g, (_, r) in enumerate(DIL_PAIRS[1:], start=1):
        proj_g = _matmul(views[g].reshape(t, D_MODEL), w_b, BF16, MM_TM, MM_TN, layer=layer,
                         col_ranges=((WB_DIL + g * dil_w, WB_DIL + (g + 1) * dil_w),))
        o, lse = _band_attention(proj_g.reshape(bsz * r, seq // r, dil_w), 0)
        dil += [o.reshape(bsz, r, seq // r, DIL_WIDTH), lse.reshape(bsz, r, seq // r, V7X_LANES)]
    y_dil = _dilated_mixture(*dil, seq)

    merged = _merge_up(y_ssd.reshape(t, SSD_INNER), y_sb.reshape(t, SB_WIDTH), y_dil, layer,
                       w_up_ssd, w_up_sb, w_up_dil, proj2d, b_gate.astype(F32).reshape(N_BRANCH, D_MODEL))
    h1, h1_p, meta, wts, counts = _out_ln(merged, layer, w_out, h, ln1_g, ln1_b, router_w, router_bias, alpha)

    pos, tile_expert, n_used, fill = _route_plan(meta, counts[:, 0].astype(jnp.int32), MOE_TM)
    y_p = _moe_experts(h1_p, pos, tile_expert, n_used, fill, layer, w_gate_e, w_up_e, w_down_e)
    return _combine_ln(h1, y_p, pos, wts.T, ln2_g, ln2_b, alpha, bsz, seq, emit_views=not last)


def kernel(x, w_in, b_gate, conv_w, conv_b, dt_bias, a_log, d_skip, ssm_norm_g, w_up_ssd, w_up_sb, w_up_dil,
           w_out, ln1_g, ln1_b, router_w, router_bias, w_gate_e, w_up_e, w_down_e, ln2_g, ln2_b):
    bsz, seq, d = x.shape
    depth = w_in.shape[0]
    alpha = (2 * depth) ** 0.25
    h = x.reshape(bsz * seq, d)
    views = _bf16_views(h, bsz, seq)
    assert w_in.shape[2] == _IN_END
    w_t = jnp.swapaxes(w_in, 1, 2)
    w_a = _cast_cols(w_t, 0, N_PROJ_A)
    w_b = _cast_cols(w_t, _IN_DT1, WB_END)
    w_dt = _cast_cols(w_t, _IN_DT0, V7X_LANES)[:, :, :SSD_HEADS]
    w_up_ssd, w_up_sb, w_up_dil, w_out, w_gate_e, w_up_e, w_down_e = (
        w.astype(BF16) for w in (w_up_ssd, w_up_sb, w_up_dil, w_out, w_gate_e, w_up_e, w_down_e))
    for l in range(depth):
        h, views = _layer(h, views, bsz, seq, alpha, l, l == depth - 1, w_a, w_b, w_dt[l],
                          b_gate[l], conv_w[l], conv_b[l], dt_bias[l], a_log[l], d_skip[l], ssm_norm_g[l],
                          w_up_ssd, w_up_sb, w_up_dil, w_out, ln1_g[l], ln1_b[l], router_w, router_bias,
                          w_gate_e, w_up_e, w_down_e, ln2_g[l], ln2_b[l])
    return h.reshape(bsz, seq, d)
```

```python
import functools

import jax
import jax.numpy as jnp
from jax import lax
from jax.experimental import pallas as pl
from jax.experimental.pallas import tpu as pltpu

F32 = jnp.float32
BF16 = jnp.bfloat16

D_MODEL = 2048
SSD_HEADS = 32
SSD_HEAD_DIM = 64
SSD_INNER = SSD_HEADS * SSD_HEAD_DIM
SSD_STATE = 128
SSD_GROUPS = 4
SSD_HEADS_PER_GROUP = SSD_HEADS // SSD_GROUPS
SSD_GROUP_WIDTH = SSD_HEADS_PER_GROUP * SSD_HEAD_DIM
SSD_CONV = 4
SSD_CHUNK = 256
SSD_BC = SSD_GROUPS * SSD_STATE
SB_HEADS = 16
SB_HEAD_DIM = 128
SB_WIDTH = SB_HEADS * SB_HEAD_DIM
DIL_PAIRS = ((128, 1), (512, 4), (2048, 16))
DIL_HEADS = 8
DIL_HEAD_DIM = 128
DIL_WIDTH = DIL_HEADS * DIL_HEAD_DIM
DIL_BLOCK = 128
N_BRANCH = 3
N_EXPERTS = 16
EXPERTS_PER_GROUP = 4
N_EXPERT_GROUPS = N_EXPERTS // EXPERTS_PER_GROUP
D_EXPERT = 1024
EPS = 1e-5
NEG = -1e30
LOG2E = 1.4426950408889634
F32_SUBNORMAL_EXP2 = -128.0

_IN_DT0 = SSD_INNER + SSD_INNER + 2 * SSD_BC
_IN_DT1 = _IN_DT0 + SSD_HEADS
_IN_END = _IN_DT1 + 3 * SB_WIDTH + 3 * len(DIL_PAIRS) * DIL_WIDTH + N_BRANCH * D_MODEL
COL_Z = 0
COL_X = SSD_INNER
COL_B = COL_X + SSD_INNER
COL_C = COL_B + SSD_BC
N_PROJ_A = COL_C + SSD_BC
WB_SB = 0
WB_DIL = WB_SB + 3 * SB_WIDTH
WB_GATE = WB_DIL + 3 * len(DIL_PAIRS) * DIL_WIDTH
WB_END = WB_GATE + N_BRANCH * D_MODEL
PB_SB = 0
PB_DIL = PB_SB + 3 * SB_WIDTH
PB_GATE = PB_DIL + 3 * DIL_WIDTH
N_PROJ_B = PB_GATE + N_BRANCH * D_MODEL
DT_LANES = 128
DT_WIDTH = SSD_GROUPS * DT_LANES

V7X_LANES = 128
V7X_VMEM_LIMIT = 48 * 1024 * 1024
MOE_VMEM_LIMIT = 56 * 1024 * 1024
MM_TM, MM_TN = 1024, 1024
CAST_TK = 1024
CAST_SPILL = 32
MERGE_TM, MERGE_TN = 512, 512
ROW_TM = 256
SB_BLOCK = 256
SB_HEADS_PER_STEP = 4
SB_EXP2_CLAMP = 64.0
MOE_TM = 256


def _params(*sem):
    return pltpu.CompilerParams(dimension_semantics=sem, vmem_limit_bytes=V7X_VMEM_LIMIT)


def _dot(a, b):
    return jnp.dot(a, b, preferred_element_type=F32)


def _dot_nt(a, b):
    return lax.dot_general(a, b, (((1,), (1,)), ((), ())), preferred_element_type=F32)


def _split_dot(x, m, passes):
    acc = None
    r = x
    for p in range(passes):
        s = r.astype(BF16)
        d = _dot(s, m)
        acc = d if acc is None else acc + d
        if p + 1 < passes:
            r = r - s.astype(F32)
    return acc


def _split_dot_left(m, x, passes):
    acc = None
    r = x
    for p in range(passes):
        s = r.astype(BF16)
        d = _dot(m, s)
        acc = d if acc is None else acc + d
        if p + 1 < passes:
            r = r - s.astype(F32)
    return acc


def _softplus(x):
    return jnp.maximum(x, 0.0) + jnp.log1p(jnp.exp(-jnp.abs(x)))


def _silu(x):
    return x * jax.nn.sigmoid(x)


def _layer_norm(y, g, b):
    mu = jnp.mean(y, axis=-1, keepdims=True)
    yc = y - mu
    var = jnp.mean(yc * yc, axis=-1, keepdims=True)
    return yc * lax.rsqrt(var + EPS) * g + b


def _pack_halves(x):
    n = x.shape[1] // 2
    bits = pltpu.bitcast(x.astype(BF16).astype(F32), jnp.uint32)
    return (bits[:, n:] & jnp.uint32(0xFFFF0000)) | (bits[:, :n] >> 16)


def _unpack_halves(p):
    lo = pltpu.bitcast(p << 16, F32)
    hi = pltpu.bitcast(p & jnp.uint32(0xFFFF0000), F32)
    return jnp.concatenate([lo, hi], axis=1)


def _start_row_gather(idx_ref, base, src_hbm, dst, sem):
    def issue(r, carry):
        pltpu.make_async_copy(src_hbm.at[pl.ds(idx_ref[base + r], 1)], dst.at[pl.ds(r, 1)], sem).start()
        return carry

    lax.fori_loop(0, dst.shape[0], issue, 0, unroll=8)


def _wait_row_gather(src_hbm, dst, sem):
    pltpu.make_async_copy(src_hbm.at[pl.ds(0, dst.shape[0])], dst, sem).wait()


def _expander(rows, cols, width):
    h = lax.broadcasted_iota(jnp.int32, (rows, cols), 0)
    j = lax.broadcasted_iota(jnp.int32, (rows, cols), 1)
    return jnp.where(h == j // width, 1.0, 0.0).astype(BF16)


def _mm_kernel(a_ref, w_ref, o_ref):
    o_ref[...] = _dot(a_ref[...], w_ref[...]).astype(o_ref.dtype)


def _matmul(a, w, out_dtype, tm, tn, col_ranges=None, layer=None):
    m, k = a.shape
    tm, tn = min(tm, m), min(tn, w.shape[-1])
    col_ranges = col_ranges or ((0, w.shape[-1]),)
    assert all(c0 % tn == 0 and c1 % tn == 0 for c0, c1 in col_ranges)
    n = sum(c1 - c0 for c0, c1 in col_ranges)

    def w_block(j):
        blk, first = None, 0
        for c0, c1 in col_ranges:
            here = c0 // tn + (j - first)
            blk = here if blk is None else jnp.where(j >= first, here, blk)
            first += (c1 - c0) // tn
        return blk

    return pl.pallas_call(
        _mm_kernel,
        out_shape=jax.ShapeDtypeStruct((m, n), out_dtype),
        grid=(m // tm, n // tn),
        in_specs=[pl.BlockSpec((tm, k), lambda i, j: (i, 0)),
                  pl.BlockSpec((k, tn), lambda i, j: (0, w_block(j))) if layer is None else
                  pl.BlockSpec((None, k, tn), lambda i, j: (layer, 0, w_block(j)))],
        out_specs=pl.BlockSpec((tm, tn), lambda i, j: (i, j)),
        compiler_params=_params("parallel", "parallel"),
        name="matmul",
    )(a, w)


def _cast_cols_kernel(a_ref, b_ref, o_ref, *, shift):
    a = a_ref[0]
    if shift:
        a = jnp.concatenate([a, b_ref[0]], axis=0)[shift:shift + a.shape[0], :]
    o_ref[0] = a.T.astype(o_ref.dtype)


def _cast_cols(w_t, col0, n_out):
    depth, n_in, k = w_t.shape
    tk = CAST_TK
    tn = min(MM_TN, n_out)
    shift = col0 % tn
    base = col0 - shift
    assert shift <= CAST_SPILL and (base + tn) % CAST_SPILL == 0 and tn % CAST_SPILL == 0
    assert n_out % tn == 0 and k % tk == 0 and col0 + n_out <= n_in
    return pl.pallas_call(
        functools.partial(_cast_cols_kernel, shift=shift),
        out_shape=jax.ShapeDtypeStruct((depth, k, n_out), BF16),
        grid=(depth, k // tk, n_out // tn),
        in_specs=[pl.BlockSpec((1, tn, tk), lambda l, i, j: (l, base // tn + j, i)),
                  pl.BlockSpec((1, CAST_SPILL, tk), lambda l, i, j: (l, (base + tn * (j + 1)) // CAST_SPILL, i))],
        out_specs=pl.BlockSpec((1, tk, tn), lambda l, i, j: (l, i, j)),
        compiler_params=_params("parallel", "parallel", "parallel"),
        name="cast_cols",
    )(w_t, w_t)


def _ssd_kernel(z_ref, x_ref, b_ref, c_ref, dt_ref, cwx_ref, cwb_ref, cwc_ref, cbx_ref, cbb_ref, cbc_ref,
                dtb_ref, alog_ref, dskip_ref, g_ref, o_ref,
                state_ref, xpad_ref, bpad_ref, cpad_ref, xs_ref, bs_ref, cs_ref, y_ref,
                dts_ref, acs_ref, ecs_ref, dte_ref):
    chunk = x_ref.shape[1]
    pad = 8

    @pl.when(pl.program_id(1) == 0)
    def _():
        state_ref[...] = jnp.zeros_like(state_ref)
        xpad_ref[0:pad, :] = jnp.zeros((pad, xpad_ref.shape[1]), F32)
        bpad_ref[0:pad, :] = jnp.zeros((pad, bpad_ref.shape[1]), F32)
        cpad_ref[0:pad, :] = jnp.zeros((pad, cpad_ref.shape[1]), F32)

    def conv_silu(src_ref, pad_ref, cw_ref, cb_ref, dst_ref):
        pad_ref[pad:pad + chunk, :] = src_ref[0].astype(F32)
        acc = cb_ref[...]
        for k in range(SSD_CONV):
            off = pad - (SSD_CONV - 1) + k
            acc = acc + cw_ref[k:k + 1, :] * pad_ref[off:off + chunk, :]
        dst_ref[...] = _silu(acc)
        pad_ref[0:pad, :] = pad_ref[chunk:chunk + pad, :]

    conv_silu(x_ref, xpad_ref, cwx_ref, cbx_ref, xs_ref)
    conv_silu(b_ref, bpad_ref, cwb_ref, cbb_ref, bs_ref)
    conv_silu(c_ref, cpad_ref, cwc_ref, cbc_ref, cs_ref)

    dt = _softplus(dt_ref[0] + dtb_ref[...])
    a_dt = dt * (-jnp.exp(alog_ref[...]))
    row = lax.broadcasted_iota(jnp.int32, (chunk, chunk), 0)
    col = lax.broadcasted_iota(jnp.int32, (chunk, chunk), 1)
    causal = col <= row
    tri = jnp.where(causal, 1.0, 0.0).astype(BF16)
    a_cs = _split_dot_left(tri, a_dt, 3)
    last = a_cs[chunk - 1:chunk, :]
    dts_ref[...] = dt
    acs_ref[...] = a_cs
    ecs_ref[...] = jnp.exp(a_cs)
    dte_ref[...] = jnp.exp(last - a_cs)

    expand = _expander(DT_LANES, SSD_GROUP_WIDTH, SSD_HEAD_DIM)
    lane = lax.broadcasted_iota(jnp.int32, (chunk, V7X_LANES), 1)
    heads_per_tile = V7X_LANES // SSD_HEAD_DIM

    def group(g, carry):
        ch0 = pl.multiple_of(g * SSD_GROUP_WIDTH, SSD_GROUP_WIDTH)
        st0 = pl.multiple_of(g * SSD_STATE, SSD_STATE)
        dt0 = pl.multiple_of(g * DT_LANES, DT_LANES)
        xg = xs_ref[:, pl.ds(ch0, SSD_GROUP_WIDTH)]
        bg = bs_ref[:, pl.ds(st0, SSD_STATE)]
        cg = cs_ref[:, pl.ds(st0, SSD_STATE)].astype(BF16)
        csg = acs_ref[:, pl.ds(dt0, DT_LANES)]
        dt_x = _split_dot(dts_ref[:, pl.ds(dt0, DT_LANES)], expand, 2)
        ecs_x = _split_dot(ecs_ref[:, pl.ds(dt0, DT_LANES)], expand, 2)
        dte_x = _split_dot(dte_ref[:, pl.ds(dt0, DT_LANES)], expand, 2)
        xdt = xg * dt_x
        xdt_b = xdt.astype(BF16)
        cb = _dot_nt(cg, bg.astype(BF16))
        cs_t = csg.T
        prev = state_ref[g]
        y_off = _dot(cg, prev.astype(BF16)) * ecs_x
        for t in range(SSD_GROUP_WIDTH // V7X_LANES):
            lanes = slice(t * V7X_LANES, (t + 1) * V7X_LANES)
            xdt_t = xdt_b[:, lanes]
            y_t = None
            for u in range(heads_per_tile):
                r = t * heads_per_tile + u
                seg = csg[:, r:r + 1] - cs_t[r:r + 1, :]
                m = (cb * jnp.exp(jnp.where(causal, seg, NEG))).astype(BF16)
                y_r = _dot(m, xdt_t)
                y_t = y_r if y_t is None else jnp.where(lane // SSD_HEAD_DIM == u, y_r, y_t)
            c0 = pl.multiple_of(ch0 + t * V7X_LANES, V7X_LANES)
            y_ref[:, pl.ds(c0, V7X_LANES)] = (y_t + y_off[:, lanes]
                                              + dskip_ref[:, pl.ds(c0, V7X_LANES)] * xg[:, lanes])
        new = ecs_x[chunk - 1:chunk, :] * prev + _dot(bg.T.astype(BF16), (xdt * dte_x).astype(BF16))
        state_ref[g] = new
        return carry

    lax.fori_loop(0, SSD_GROUPS, group, 0)

    yf = y_ref[...] * _silu(z_ref[0].astype(F32))
    ms = jnp.mean(yf * yf, axis=-1, keepdims=True)
    o_ref[0] = (yf * lax.rsqrt(ms + EPS) * g_ref[...]).astype(o_ref.dtype)


def _ssd(proj, dt_raw, conv_w, conv_b, dt_bias, a_log, d_skip, norm_g):
    bsz, seq, _ = proj.shape
    chunk = min(SSD_CHUNK, seq)
    nc = seq // chunk

    def spread_heads(v):
        v = v.astype(F32).reshape(SSD_GROUPS, SSD_HEADS_PER_GROUP)
        return jnp.pad(v, ((0, 0), (0, DT_LANES - SSD_HEADS_PER_GROUP))).reshape(1, DT_WIDTH)

    cw = conv_w.astype(F32)
    cbias = conv_b.astype(F32).reshape(1, -1)
    x1, b1 = SSD_INNER, SSD_INNER + SSD_BC
    consts = [cw[:, :x1], cw[:, x1:b1], cw[:, b1:], cbias[:, :x1], cbias[:, x1:b1], cbias[:, b1:],
              spread_heads(dt_bias), spread_heads(a_log),
              jnp.repeat(d_skip.astype(F32), SSD_HEAD_DIM).reshape(1, SSD_INNER),
              norm_g.astype(F32).reshape(1, SSD_INNER)]

    def col_block(width, col):
        return pl.BlockSpec((1, chunk, width), lambda b, c: (b, c, col // width))

    def whole(a):
        return pl.BlockSpec(a.shape, lambda b, c: (0, 0))

    return pl.pallas_call(
        _ssd_kernel,
        out_shape=jax.ShapeDtypeStruct((bsz, seq, SSD_INNER), BF16),
        grid=(bsz, nc),
        in_specs=[col_block(SSD_INNER, COL_Z), col_block(SSD_INNER, COL_X),
                  col_block(SSD_BC, COL_B), col_block(SSD_BC, COL_C),
                  pl.BlockSpec((1, chunk, DT_WIDTH), lambda b, c: (b, c, 0))] + [whole(a) for a in consts],
        out_specs=pl.BlockSpec((1, chunk, SSD_INNER), lambda b, c: (b, c, 0)),
        scratch_shapes=[pltpu.VMEM((SSD_GROUPS, SSD_STATE, SSD_GROUP_WIDTH), F32),
                        pltpu.VMEM((chunk + 8, SSD_INNER), F32),
                        pltpu.VMEM((chunk + 8, SSD_BC), F32),
                        pltpu.VMEM((chunk + 8, SSD_BC), F32),
                        pltpu.VMEM((chunk, SSD_INNER), F32),
                        pltpu.VMEM((chunk, SSD_BC), F32),
                        pltpu.VMEM((chunk, SSD_BC), F32),
                        pltpu.VMEM((chunk, SSD_INNER), F32),
                        pltpu.VMEM((chunk, DT_WIDTH), F32),
                        pltpu.VMEM((chunk, DT_WIDTH), F32),
                        pltpu.VMEM((chunk, DT_WIDTH), F32),
                        pltpu.VMEM((chunk, DT_WIDTH), F32)],
        compiler_params=_params("parallel", "arbitrary"),
        name="ssd",
    )(proj, proj, proj, proj, dt_raw, *consts)


def _sb_kernel(q_ref, k_ref, v_ref, o_ref, *, blk, scale2):
    qi = pl.program_id(2)
    hd = SB_HEAD_DIM
    row = lax.broadcasted_iota(jnp.int32, (blk, blk), 0)
    col = lax.broadcasted_iota(jnp.int32, (blk, blk), 1)
    strict = col < row
    upper = jnp.where(row >= col, 1.0, 0.0).astype(BF16)
    upper2 = jnp.concatenate([upper, upper], axis=0)

    lanes = [slice(h * hd, (h + 1) * hd) for h in range(q_ref.shape[2] // hd)]

    def blocks(kb, rs, diagonal):
        start = pl.multiple_of(kb * blk, blk)
        zs = [_dot_nt(q_ref[0, :, sl], k_ref[0, pl.ds(start, blk), sl]) * scale2 for sl in lanes]
        sufs = []
        for z in zs:
            sp = jnp.maximum(z, jnp.log2(1.0 + jnp.exp2(jnp.minimum(z, SB_EXP2_CLAMP))))
            if diagonal:
                sp = jnp.where(strict, sp, 0.0)
            hi = pltpu.bitcast(pltpu.bitcast(sp, jnp.uint32) & jnp.uint32(0xFFFF0000), F32)
            parts = jnp.concatenate([hi.astype(BF16), (sp - hi).astype(BF16)], axis=1)
            sufs.append(_dot(parts, upper2))
        new_rs = tuple(r - suf[:, 0:1] for r, suf in zip(rs, sufs))
        pvs = []
        for z, suf, r, sl in zip(zs, sufs, rs, lanes):
            w = jnp.exp2(z - suf + r)
            if diagonal:
                w = jnp.where(strict, w, 0.0)
            pvs.append(_dot(w.astype(BF16), v_ref[0, pl.ds(start, blk), sl]))
        return new_rs, pvs

    def largest(rs):
        m = rs[0]
        for r in rs[1:]:
            m = jnp.maximum(m, r)
        return jnp.max(m)

    rs, accs = blocks(qi, tuple(jnp.zeros((blk, 1), F32) for _ in lanes), True)

    def more(c):
        return (c[0] >= 0) & (c[1] > F32_SUBNORMAL_EXP2)

    def body(c):
        kb, _, rs, accs = c
        rs, pvs = blocks(kb, rs, False)
        return kb - 1, largest(rs), rs, tuple(a + pv for a, pv in zip(accs, pvs))

    _, _, _, accs = lax.while_loop(more, body, (qi - 1, largest(rs), rs, tuple(accs)))
    o_ref[0] = jnp.concatenate(accs, axis=1).astype(o_ref.dtype)


def _stick_breaking(proj):
    bsz, seq, _ = proj.shape
    blk = min(SB_BLOCK, seq)
    qc, kc, vc = (PB_SB + i * SB_WIDTH for i in range(3))
    w = SB_HEADS_PER_STEP * SB_HEAD_DIM
    return pl.pallas_call(
        functools.partial(_sb_kernel, blk=blk, scale2=SB_HEAD_DIM ** -0.5 * LOG2E),
        out_shape=jax.ShapeDtypeStruct((bsz, seq, SB_WIDTH), BF16),
        grid=(bsz, SB_WIDTH // w, seq // blk),
        in_specs=[pl.BlockSpec((1, blk, w), lambda b, h, i: (b, i, qc // w + h)),
                  pl.BlockSpec((1, seq, w), lambda b, h, i: (b, 0, kc // w + h)),
                  pl.BlockSpec((1, seq, w), lambda b, h, i: (b, 0, vc // w + h))],
        out_specs=pl.BlockSpec((1, blk, w), lambda b, h, i: (b, i, h)),
        compiler_params=_params("parallel", "parallel", "arbitrary"),
        name="stick_breaking",
    )(proj, proj, proj)


def _dil_kernel(q_ref, kp_ref, kc_ref, vp_ref, vc_ref, o_ref, lse_ref, *, scale):
    bq = DIL_BLOCK
    blk = pl.program_id(1)
    i = lax.broadcasted_iota(jnp.int32, (bq, 2 * bq), 0)
    m = lax.broadcasted_iota(jnp.int32, (bq, 2 * bq), 1)
    dist = i - m + bq
    first = jnp.where(blk > 0, 0, bq)
    valid = (dist >= 0) & (dist <= bq) & (m >= first)
    lane = lax.broadcasted_iota(jnp.int32, (bq, V7X_LANES), 1)
    lse_all = jnp.zeros((bq, V7X_LANES), F32)
    for h in range(DIL_HEADS):
        sl = slice(h * DIL_HEAD_DIM, (h + 1) * DIL_HEAD_DIM)
        k = jnp.concatenate([kp_ref[0, :, sl], kc_ref[0, :, sl]], axis=0)
        v = jnp.concatenate([vp_ref[0, :, sl], vc_ref[0, :, sl]], axis=0)
        s = jnp.where(valid, _dot_nt(q_ref[0, :, sl], k) * scale, NEG)
        mx = jnp.max(s, axis=-1, keepdims=True)
        p = jnp.exp(s - mx)
        den = jnp.sum(p, axis=-1, keepdims=True)
        o_ref[0, :, sl] = (_dot(p.astype(BF16), v) / den).astype(o_ref.dtype)
        lse_all = jnp.where(lane == h, mx + jnp.log(den), lse_all)
    lse_ref[0] = lse_all


def _band_attention(qkv, col):
    nz, lc, _ = qkv.shape
    bq = DIL_BLOCK
    w = DIL_WIDTH
    assert lc % bq == 0 and col % w == 0

    def cur(part):
        return pl.BlockSpec((1, bq, w), lambda z, i: (z, i, col // w + part))

    def prev(part):
        return pl.BlockSpec((1, bq, w), lambda z, i: (z, jnp.maximum(i - 1, 0), col // w + part))

    return pl.pallas_call(
        functools.partial(_dil_kernel, scale=DIL_HEAD_DIM ** -0.5),
        out_shape=(jax.ShapeDtypeStruct((nz, lc, w), BF16),
                   jax.ShapeDtypeStruct((nz, lc, V7X_LANES), F32)),
        grid=(nz, lc // bq),
        in_specs=[cur(0), prev(1), cur(1), prev(2), cur(2)],
        out_specs=(pl.BlockSpec((1, bq, w), lambda z, i: (z, i, 0)),
                   pl.BlockSpec((1, bq, V7X_LANES), lambda z, i: (z, i, 0))),
        compiler_params=_params("parallel", "arbitrary"),
        name="dilated_attention",
    )(qkv, qkv, qkv, qkv, qkv)


def _residue_block(tm, seq, dilation, width):
    tiles = seq // tm
    return pl.BlockSpec((1, dilation, tm // dilation, width), lambda i, *_: (i // tiles, 0, i % tiles, 0))


def _emit_by_residue(x, strip_ref, dst_refs):
    tm, w = x.shape
    strips = w // V7X_LANES
    for s in range(strips):
        strip_ref[s] = x[:, s * V7X_LANES:(s + 1) * V7X_LANES]
    for dst in dst_refs:
        d, n = dst.shape[1], dst.shape[2]
        for c in range(d):
            for s in range(strips):
                dst[0, c, :, s * V7X_LANES:(s + 1) * V7X_LANES] = strip_ref[s, pl.ds(c, n, stride=d), :].astype(dst.dtype)


def _dilmix_kernel(o0_ref, l0_ref, o1_ref, l1_ref, o2_ref, l2_ref, out_ref, strip_ref, lse_ref):
    tm = out_ref.shape[0]
    heads = DIL_WIDTH // DIL_HEAD_DIM
    for gi, (o_ref, l_ref) in enumerate(((o1_ref, l1_ref), (o2_ref, l2_ref))):
        d = o_ref.shape[1]
        n = tm // d
        for c in range(d):
            lse_ref[gi, pl.ds(c, n, stride=d), :] = l_ref[0, c]
            for j in range(heads):
                strip_ref[gi, j, pl.ds(c, n, stride=d), :] = (
                    o_ref[0, c, :, j * DIL_HEAD_DIM:(j + 1) * DIL_HEAD_DIM].astype(F32))
    ls = [l0_ref[...], lse_ref[0], lse_ref[1]]
    mx = jnp.maximum(jnp.maximum(ls[0], ls[1]), ls[2])
    es = [jnp.exp(l - mx) for l in ls]
    inv = 1.0 / (es[0] + es[1] + es[2])
    ws = [e * inv for e in es]
    for j in range(heads):
        sl = slice(j * DIL_HEAD_DIM, (j + 1) * DIL_HEAD_DIM)
        acc = (ws[0][:, j:j + 1] * o0_ref[:, sl].astype(F32) + ws[1][:, j:j + 1] * strip_ref[0, j]
               + ws[2][:, j:j + 1] * strip_ref[1, j])
        out_ref[:, sl] = acc.astype(out_ref.dtype)


def _dilated_mixture(o0, l0, o1, l1, o2, l2, seq):
    t = o0.shape[0]
    tm = min(ROW_TM, t)
    heads = DIL_WIDTH // DIL_HEAD_DIM
    assert DIL_HEAD_DIM == V7X_LANES
    o_spec = pl.BlockSpec((tm, DIL_WIDTH), lambda i: (i, 0))
    l_spec = pl.BlockSpec((tm, V7X_LANES), lambda i: (i, 0))
    res = [spec for o in (o1, o2) for spec in (_residue_block(tm, seq, o.shape[1], DIL_WIDTH),
                                               _residue_block(tm, seq, o.shape[1], V7X_LANES))]
    return pl.pallas_call(
        _dilmix_kernel,
        out_shape=jax.ShapeDtypeStruct((t, DIL_WIDTH), BF16),
        grid=(t // tm,),
        in_specs=[o_spec, l_spec] + res,
        out_specs=o_spec,
        scratch_shapes=[pltpu.VMEM((2, heads, tm, DIL_HEAD_DIM), F32), pltpu.VMEM((2, tm, V7X_LANES), F32)],
        compiler_params=_params("parallel"),
        name="dilated_mix",
    )(o0, l0, o1, l1, o2, l2)


def _merge_kernel(ya_ref, yb_ref, yc_ref, ua_ref, ub_ref, uc_ref, ga_ref, gb_ref, gc_ref, bg_ref, o_ref):
    acc = None
    for i, (y_ref, u_ref, g_ref) in enumerate(((ya_ref, ua_ref, ga_ref), (yb_ref, ub_ref, gb_ref),
                                               (yc_ref, uc_ref, gc_ref))):
        gate = jax.nn.sigmoid(g_ref[...].astype(F32) + bg_ref[i:i + 1, :])
        term = gate * _dot(y_ref[...], u_ref[...])
        acc = term if acc is None else acc + term
    o_ref[...] = acc.astype(o_ref.dtype)


def _merge_up(y_ssd, y_sb, y_dil, layer, u_ssd, u_sb, u_dil, proj2d, b_gate):
    t = y_ssd.shape[0]
    tm, tn = min(MERGE_TM, t), MERGE_TN

    def rows(a):
        return pl.BlockSpec((tm, a.shape[1]), lambda i, j: (i, 0))

    def cols(a):
        return pl.BlockSpec((None, a.shape[1], tn), lambda i, j: (layer, 0, j))

    def gate(k):
        return pl.BlockSpec((tm, tn), lambda i, j: (i, (PB_GATE + k * D_MODEL) // tn + j))

    return pl.pallas_call(
        _merge_kernel,
        out_shape=jax.ShapeDtypeStruct((t, D_MODEL), BF16),
        grid=(t // tm, D_MODEL // tn),
        in_specs=[rows(y_ssd), rows(y_sb), rows(y_dil), cols(u_ssd), cols(u_sb), cols(u_dil),
                  gate(0), gate(1), gate(2), pl.BlockSpec((N_BRANCH, tn), lambda i, j: (0, j))],
        out_specs=pl.BlockSpec((tm, tn), lambda i, j: (i, j)),
        compiler_params=_params("parallel", "parallel"),
        name="merge_up",
    )(y_ssd, y_sb, y_dil, u_ssd, u_sb, u_dil, proj2d, proj2d, proj2d, b_gate)


def _outln_kernel(m_ref, w_ref, h_ref, g_ref, b_ref, rwh_ref, rwl_ref, rb_ref, of_ref, op_ref, meta_ref, wts_ref,
                  cnt_ref, *, alpha):
    @pl.when(pl.program_id(0) == 0)
    def _():
        cnt_ref[...] = jnp.zeros_like(cnt_ref)

    y = alpha * h_ref[...] + _dot(m_ref[...], w_ref[...])
    out = _layer_norm(y, g_ref[...], b_ref[...])
    of_ref[...] = out
    op_ref[...] = _pack_halves(out)
    e1, e2, w1, w2 = _route(out, rwh_ref[...], rwl_ref[...], rb_ref[...])
    r1, r2 = _expert_ranks(e1, e2, cnt_ref)
    for row, v in enumerate((e1, e2, r1, r2)):
        meta_ref[row:row + 1, :] = v
    wts_ref[0:1, :] = w1
    wts_ref[1:2, :] = w2


def _out_ln(merged, layer, w_out, h, g, b, router_w, router_bias, alpha):
    t, d = h.shape
    tm = min(ROW_TM, t)
    row = pl.BlockSpec((tm, d), lambda i: (i, 0))
    vec = pl.BlockSpec((1, d), lambda i: (0, 0))
    rw = jnp.pad(router_w.astype(F32), ((0, 0), (0, V7X_LANES - N_EXPERTS)))
    rw_hi = rw.astype(BF16)
    rw_lo = (rw - rw_hi.astype(F32)).astype(BF16)
    rw_spec = pl.BlockSpec((d, V7X_LANES), lambda i: (0, 0))
    return pl.pallas_call(
        functools.partial(_outln_kernel, alpha=alpha),
        out_shape=(jax.ShapeDtypeStruct((t, d), F32), jax.ShapeDtypeStruct((t, d // 2), jnp.uint32),
                   jax.ShapeDtypeStruct((4, t), jnp.int32), jax.ShapeDtypeStruct((2, t), F32),
                   jax.ShapeDtypeStruct((N_EXPERTS, V7X_LANES), F32)),
        grid=(t // tm,),
        in_specs=[row, pl.BlockSpec((None, d, d), lambda i: (layer, 0, 0)), row, vec, vec,
                  rw_spec, rw_spec, pl.BlockSpec((N_EXPERTS, 1), lambda i: (0, 0))],
        out_specs=(row, pl.BlockSpec((tm, d // 2), lambda i: (i, 0)),
                   pl.BlockSpec((4, tm), lambda i: (0, i)), pl.BlockSpec((2, tm), lambda i: (0, i)),
                   pl.BlockSpec((N_EXPERTS, V7X_LANES), lambda i: (0, 0))),
        compiler_params=_params("arbitrary"),
        name="out_ln",
    )(merged, w_out, h, g.reshape(1, d), b.reshape(1, d),
      rw_hi, rw_lo, router_bias.astype(F32).reshape(N_EXPERTS, 1))


def _route(h, rw_hi, rw_lo, rb):
    h_hi = h.astype(BF16)
    h_lo = (h - h_hi.astype(F32)).astype(BF16)
    logits = _dot(h_hi, rw_hi) + (_dot(h_lo, rw_hi) + _dot(h_hi, rw_lo))
    logits = logits.T[0:N_EXPERTS, :]
    mx = jnp.max(logits, axis=0, keepdims=True)
    ex = jnp.exp(logits - mx)
    aff = ex / jnp.sum(ex, axis=0, keepdims=True)
    sel = aff + rb
    s = [sel[e:e + 1, :] for e in range(N_EXPERTS)]
    a = [aff[e:e + 1, :] for e in range(N_EXPERTS)]
    n = EXPERTS_PER_GROUP
    best_score, best = None, None
    for g in range(N_EXPERT_GROUPS):
        grp = s[g * n:(g + 1) * n]
        score = None
        for i in range(n):
            for j in range(i + 1, n):
                pair = grp[i] + grp[j]
                score = pair if score is None else jnp.maximum(score, pair)
        if best is None:
            best_score, best = score, jnp.zeros_like(score, dtype=jnp.int32)
        else:
            take = score > best_score
            best_score = jnp.where(take, score, best_score)
            best = jnp.where(take, g, best)
    sv, av = [], []
    for r in range(n):
        sr, ar = s[r], a[r]
        for g in range(1, N_EXPERT_GROUPS):
            sr = jnp.where(best == g, s[g * n + r], sr)
            ar = jnp.where(best == g, a[g * n + r], ar)
        sv.append(sr)
        av.append(ar)

    def arg_top(vals, skip):
        top_v, top_i, top_a = None, None, None
        for r in range(n):
            v = vals[r] if skip is None else jnp.where(skip == r, -jnp.inf, vals[r])
            if top_v is None:
                top_v, top_i, top_a = v, jnp.zeros_like(best), av[0]
            else:
                take = v > top_v
                top_v = jnp.where(take, v, top_v)
                top_i = jnp.where(take, r, top_i)
                top_a = jnp.where(take, av[r], top_a)
        return top_i, top_a

    i1, a1 = arg_top(sv, None)
    i2, a2 = arg_top(sv, i1)
    tot = a1 + a2
    return best * n + i1, best * n + i2, a1 / tot, a2 / tot


def _expert_ranks(e1, e2, count_ref):
    tm = e1.shape[1]
    expert = lax.broadcasted_iota(jnp.int32, (N_EXPERTS, tm), 0)
    hit1, hit2 = expert == e1, expert == e2
    both = jnp.where(hit1, 1.0, 0.0) + jnp.where(hit2, 1.0, 0.0)
    earlier = lax.broadcasted_iota(jnp.int32, (tm, tm), 0) < lax.broadcasted_iota(jnp.int32, (tm, tm), 1)
    before = _dot(both.astype(BF16), jnp.where(earlier, 1.0, 0.0).astype(BF16)) + count_ref[:, 0:1]
    r1 = jnp.sum(jnp.where(hit1, before, 0.0), axis=0, keepdims=True)
    r2 = jnp.sum(jnp.where(hit2, before, 0.0), axis=0, keepdims=True)
    count_ref[...] = count_ref[...] + jnp.sum(both, axis=1, keepdims=True)
    return r1.astype(jnp.int32), r2.astype(jnp.int32)


def _moe_kernel(te_ref, nu_ref, fill_ref, next_ref, pos_ref, x_hbm, wg_hbm, wu_hbm, wd_hbm, o_ref,
                xbuf, sem, src_ref, stage_g, stage_u, stage_d, wbf_g, wbf_u, wbf_d, wsem, *, layer):
    i = pl.program_id(0)
    n_used = nu_ref[0]
    tm = xbuf.shape[1]
    n_tokens = pos_ref.shape[0] // 2
    w_hbm = (wg_hbm, wu_hbm, wd_hbm)
    stage = (stage_g, stage_u, stage_d)
    wbf = (wbf_g, wbf_u, wbf_d)

    def weight_copy(which, expert):
        return pltpu.make_async_copy(w_hbm[which].at[layer, expert], stage[which], wsem.at[which])

    @pl.when(i == 0)
    def _():
        for which in range(3):
            weight_copy(which, te_ref[0]).start()

    @pl.when((i < n_used) & ((i == 0) | (te_ref[jnp.maximum(i - 1, 0)] != te_ref[i])))
    def _():
        for which in range(3):
            weight_copy(which, te_ref[i]).wait()
            wbf[which][...] = stage[which][...].astype(BF16)

        @pl.when(next_ref[i] >= 0)
        def _():
            for which in range(3):
                weight_copy(which, next_ref[i]).start()

    @pl.when(i == 0)
    def _():
        def pad_expert(e, carry):
            def pad_slot(s, c):
                src_ref[s] = 0
                return c

            return lax.fori_loop(fill_ref[e], fill_ref[N_EXPERTS + e], pad_slot, carry)

        lax.fori_loop(0, N_EXPERTS, pad_expert, 0)
        for k in range(2):
            def place(tok, carry):
                src_ref[pos_ref[k * n_tokens + tok]] = tok
                return carry

            lax.fori_loop(0, n_tokens, place, 0, unroll=8)

    def fetch(tile):
        slot = tile % 2
        _start_row_gather(src_ref, tile * tm, x_hbm, xbuf.at[slot], sem.at[slot])

    @pl.when(i == 0)
    def _():
        fetch(i)

    @pl.when(i + 1 < n_used)
    def _():
        fetch(i + 1)

    @pl.when(i < n_used)
    def _():
        slot = i % 2
        _wait_row_gather(x_hbm, xbuf.at[slot], sem.at[slot])
        x = _unpack_halves(xbuf[slot]).astype(BF16)
        act = _silu(_dot(x, wbf_g[...])) * _dot(x, wbf_u[...])
        o_ref[...] = _pack_halves(_dot(act.astype(BF16), wbf_d[...]))

    @pl.when(i >= n_used)
    def _():
        o_ref[...] = jnp.zeros_like(o_ref)


def _route_plan(meta, counts, tm):
    t = meta.shape[1]
    padded = (counts + tm - 1) // tm * tm
    ends = jnp.cumsum(padded)
    starts = ends - padded
    one_hot = meta[0:2, :, None] == jnp.arange(N_EXPERTS, dtype=jnp.int32)
    pos = jnp.sum(jnp.where(one_hot, starts, 0), axis=-1) + meta[2:4]
    n_tiles = (2 * t + N_EXPERTS * tm) // tm
    n_used = (ends[-1] // tm).astype(jnp.int32)
    tile_start = jnp.minimum(jnp.arange(n_tiles, dtype=jnp.int32), n_used - 1) * tm
    owner = jnp.sum((ends[None, :] <= tile_start[:, None]).astype(jnp.int32), axis=1)
    owner = jnp.minimum(owner, N_EXPERTS - 1)
    fill = jnp.concatenate([starts + counts, ends]).astype(jnp.int32)
    expert = jnp.arange(N_EXPERTS, dtype=jnp.int32)
    later = (expert[None, :] > owner[:, None]) & (counts[None, :] > 0)
    next_expert = jnp.where(jnp.any(later, axis=1), jnp.argmax(later, axis=1).astype(jnp.int32), -1)
    return pos.reshape(-1), owner, n_used.reshape(1), fill, next_expert


def _moe_experts(x_packed, pos, tile_expert, n_used, fill, next_expert, layer, w_gate, w_up, w_down):
    n_tiles = tile_expert.shape[0]
    dp = x_packed.shape[1]
    tm = MOE_TM
    d, de = w_gate.shape[2], w_gate.shape[3]
    any_space = pl.BlockSpec(memory_space=pl.ANY)
    return pl.pallas_call(
        functools.partial(_moe_kernel, layer=layer),
        out_shape=jax.ShapeDtypeStruct((n_tiles * tm, dp), jnp.uint32),
        grid_spec=pltpu.PrefetchScalarGridSpec(
            num_scalar_prefetch=5,
            grid=(n_tiles,),
            in_specs=[any_space] * 4,
            out_specs=pl.BlockSpec((tm, dp), lambda i, *_: (i, 0)),
            scratch_shapes=[pltpu.VMEM((2, tm, dp), jnp.uint32), pltpu.SemaphoreType.DMA((2,)),
                            pltpu.SMEM((n_tiles * tm,), jnp.int32),
                            pltpu.VMEM((d, de), F32), pltpu.VMEM((d, de), F32), pltpu.VMEM((de, d), F32),
                            pltpu.VMEM((d, de), BF16), pltpu.VMEM((d, de), BF16), pltpu.VMEM((de, d), BF16),
                            pltpu.SemaphoreType.DMA((3,))]),
        compiler_params=pltpu.CompilerParams(dimension_semantics=("arbitrary",), vmem_limit_bytes=MOE_VMEM_LIMIT),
        name="moe_experts",
    )(tile_expert, n_used, fill, next_expert, pos, x_packed, w_gate, w_up, w_down)


def _ln2_kernel(pos_ref, h_ref, y_hbm, w_ref, g_ref, b_ref, of_ref, *rest, alpha, n_views):
    views, (rbuf, sem), strips = rest[:n_views], rest[n_views:n_views + 2], rest[n_views + 2:]
    i = pl.program_id(0)
    n_tiles = pl.num_programs(0)
    tm = h_ref.shape[0]

    def fetch(tile):
        slot = tile % 2
        for k in range(2):
            _start_row_gather(pos_ref, (k * n_tiles + tile) * tm, y_hbm, rbuf.at[slot, k], sem.at[slot, k])

    @pl.when(i == 0)
    def _():
        fetch(i)

    @pl.when(i + 1 < n_tiles)
    def _():
        fetch(i + 1)

    slot = i % 2
    y = alpha * h_ref[...]
    w = w_ref[...]
    for k in range(2):
        _wait_row_gather(y_hbm, rbuf.at[slot, k], sem.at[slot, k])
        y = y + w[:, k:k + 1] * _unpack_halves(rbuf[slot, k])
    out = _layer_norm(y, g_ref[...], b_ref[...])
    of_ref[...] = out
    if views:
        views[0][...] = out.astype(BF16)
        _emit_by_residue(out, strips[0], views[1:])


def _view_shapes_specs(bsz, seq, d, tm):
    shapes = [jax.ShapeDtypeStruct((bsz * seq, d), BF16)]
    specs = [pl.BlockSpec((tm, d), lambda i, *_: (i, 0))]
    for _, r in DIL_PAIRS[1:]:
        shapes.append(jax.ShapeDtypeStruct((bsz, r, seq // r, d), BF16))
        specs.append(_residue_block(tm, seq, r, d))
    return shapes, specs


def _combine_ln(h, y_packed, pos_by_choice, wts, g, b, alpha, bsz, seq, emit_views):
    t, d = h.shape
    dp = y_packed.shape[1]
    tm = min(ROW_TM, t)
    row = pl.BlockSpec((tm, d), lambda i, pos: (i, 0))
    vec = pl.BlockSpec((1, d), lambda i, pos: (0, 0))
    v_shapes, v_specs = _view_shapes_specs(bsz, seq, d, tm) if emit_views else ([], [])
    scratch = [pltpu.VMEM((2, 2, tm, dp), jnp.uint32), pltpu.SemaphoreType.DMA((2, 2))]
    if emit_views:
        scratch.append(pltpu.VMEM((d // V7X_LANES, tm, V7X_LANES), F32))
    out = pl.pallas_call(
        functools.partial(_ln2_kernel, alpha=alpha, n_views=len(v_shapes)),
        out_shape=[jax.ShapeDtypeStruct((t, d), F32)] + v_shapes,
        grid_spec=pltpu.PrefetchScalarGridSpec(
            num_scalar_prefetch=1,
            grid=(t // tm,),
            in_specs=[row, pl.BlockSpec(memory_space=pl.ANY), pl.BlockSpec((tm, 2), lambda i, pos: (i, 0)), vec, vec],
            out_specs=[row] + v_specs,
            scratch_shapes=scratch),
        compiler_params=_params("arbitrary"),
        name="combine_ln",
    )(pos_by_choice, h, y_packed, wts, g.reshape(1, d), b.reshape(1, d))
    return out[0], tuple(out[1:])


def _views_kernel(x_ref, *rest):
    views, strip_ref = rest[:-1], rest[-1]
    x = x_ref[...]
    views[0][...] = x.astype(BF16)
    _emit_by_residue(x, strip_ref, views[1:])


def _bf16_views(x2d, bsz, seq):
    t, d = x2d.shape
    tm = min(ROW_TM, t)
    v_shapes, v_specs = _view_shapes_specs(bsz, seq, d, tm)
    return tuple(pl.pallas_call(
        _views_kernel,
        out_shape=v_shapes,
        grid=(t // tm,),
        in_specs=[pl.BlockSpec((tm, d), lambda i: (i, 0))],
        out_specs=v_specs,
        scratch_shapes=[pltpu.VMEM((d // V7X_LANES, tm, V7X_LANES), F32)],
        compiler_params=_params("parallel"),
        name="bf16_views",
    )(x2d))


def _layer(h, views, bsz, seq, alpha, layer, last, w_a, w_b, w_dt, b_gate, conv_w, conv_b, dt_bias, a_log, d_skip,
           ssm_norm_g, w_up_ssd, w_up_sb, w_up_dil, w_out, ln1_g, ln1_b, router_w, router_bias,
           w_gate_e, w_up_e, w_down_e, ln2_g, ln2_b):
    t = bsz * seq
    h_b = views[0]
    dil_w = 3 * DIL_WIDTH
    w_dt = w_dt.reshape(D_MODEL, SSD_GROUPS, SSD_HEADS_PER_GROUP)
    w_dt = jnp.pad(w_dt, ((0, 0), (0, 0), (0, DT_LANES - SSD_HEADS_PER_GROUP))).reshape(D_MODEL, DT_WIDTH)
    proj_a = _matmul(h_b, w_a, BF16, MM_TM, MM_TN, layer=layer).reshape(bsz, seq, N_PROJ_A)
    proj2d = _matmul(h_b, w_b, BF16, MM_TM, MM_TN, layer=layer,
                     col_ranges=((WB_SB, WB_DIL + dil_w), (WB_GATE, WB_END)))
    dt_raw = _matmul(h_b, w_dt, F32, MM_TM, DT_WIDTH)
    proj = proj2d.reshape(bsz, seq, N_PROJ_B)

    y_ssd = _ssd(proj_a, dt_raw.reshape(bsz, seq, DT_WIDTH), conv_w, conv_b, dt_bias, a_log, d_skip, ssm_norm_g)
    y_sb = _stick_breaking(proj)

    assert DIL_PAIRS[0][1] == 1 and all(w // d == DIL_BLOCK for w, d in DIL_PAIRS)
    o, lse = _band_attention(proj, PB_DIL)
    dil = [o.reshape(t, DIL_WIDTH), lse.reshape(t, V7X_LANES)]
    for g, (_, r) in enumerate(DIL_PAIRS[1:], start=1):
        proj_g = _matmul(views[g].reshape(t, D_MODEL), w_b, BF16, MM_TM, MM_TN, layer=layer,
                         col_ranges=((WB_DIL + g * dil_w, WB_DIL + (g + 1) * dil_w),))
        o, lse = _band_attention(proj_g.reshape(bsz * r, seq // r, dil_w), 0)
        dil += [o.reshape(bsz, r, seq // r, DIL_WIDTH), lse.reshape(bsz, r, seq // r, V7X_LANES)]
    y_dil = _dilated_mixture(*dil, seq)

    merged = _merge_up(y_ssd.reshape(t, SSD_INNER), y_sb.reshape(t, SB_WIDTH), y_dil, layer,
                       w_up_ssd, w_up_sb, w_up_dil, proj2d, b_gate.astype(F32).reshape(N_BRANCH, D_MODEL))
    h1, h1_p, meta, wts, counts = _out_ln(merged, layer, w_out, h, ln1_g, ln1_b, router_w, router_bias, alpha)

    pos, tile_expert, n_used, fill, next_expert = _route_plan(meta, counts[:, 0].astype(jnp.int32), MOE_TM)
    y_p = _moe_experts(h1_p, pos, tile_expert, n_used, fill, next_expert, layer, w_gate_e, w_up_e, w_down_e)
    return _combine_ln(h1, y_p, pos, wts.T, ln2_g, ln2_b, alpha, bsz, seq, emit_views=not last)


def kernel(x, w_in, b_gate, conv_w, conv_b, dt_bias, a_log, d_skip, ssm_norm_g, w_up_ssd, w_up_sb, w_up_dil,
           w_out, ln1_g, ln1_b, router_w, router_bias, w_gate_e, w_up_e, w_down_e, ln2_g, ln2_b):
    bsz, seq, d = x.shape
    depth = w_in.shape[0]
    alpha = (2 * depth) ** 0.25
    h = x.reshape(bsz * seq, d)
    views = _bf16_views(h, bsz, seq)
    assert w_in.shape[2] == _IN_END
    w_t = jnp.swapaxes(w_in, 1, 2)
    w_a = _cast_cols(w_t, 0, N_PROJ_A)
    w_b = _cast_cols(w_t, _IN_DT1, WB_END)
    w_dt = _cast_cols(w_t, _IN_DT0, V7X_LANES)[:, :, :SSD_HEADS]
    w_up_ssd, w_up_sb, w_up_dil, w_out = (w.astype(BF16) for w in (w_up_ssd, w_up_sb, w_up_dil, w_out))
    for l in range(depth):
        h, views = _layer(h, views, bsz, seq, alpha, l, l == depth - 1, w_a, w_b, w_dt[l],
                          b_gate[l], conv_w[l], conv_b[l], dt_bias[l], a_log[l], d_skip[l], ssm_norm_g[l],
                          w_up_ssd, w_up_sb, w_up_dil, w_out, ln1_g[l], ln1_b[l], router_w, router_bias,
                          w_gate_e, w_up_e, w_down_e, ln2_g[l], ln2_b[l])
    return h.reshape(bsz, seq, d)
```

```python
import functools

import jax
import jax.numpy as jnp
from jax import lax
from jax.experimental import pallas as pl
from jax.experimental.pallas import tpu as pltpu

F32 = jnp.float32
BF16 = jnp.bfloat16

D_MODEL = 2048
SSD_HEADS = 32
SSD_HEAD_DIM = 64
SSD_INNER = SSD_HEADS * SSD_HEAD_DIM
SSD_STATE = 128
SSD_GROUPS = 4
SSD_HEADS_PER_GROUP = SSD_HEADS // SSD_GROUPS
SSD_GROUP_WIDTH = SSD_HEADS_PER_GROUP * SSD_HEAD_DIM
SSD_CONV = 4
SSD_CHUNK = 256
SSD_BC = SSD_GROUPS * SSD_STATE
SB_HEADS = 16
SB_HEAD_DIM = 128
SB_WIDTH = SB_HEADS * SB_HEAD_DIM
DIL_PAIRS = ((128, 1), (512, 4), (2048, 16))
DIL_HEADS = 8
DIL_HEAD_DIM = 128
DIL_WIDTH = DIL_HEADS * DIL_HEAD_DIM
DIL_BLOCK = 128
N_BRANCH = 3
N_EXPERTS = 16
EXPERTS_PER_GROUP = 4
N_EXPERT_GROUPS = N_EXPERTS // EXPERTS_PER_GROUP
D_EXPERT = 1024
EPS = 1e-5
NEG = -1e30
LOG2E = 1.4426950408889634
F32_SUBNORMAL_EXP2 = -128.0

_IN_DT0 = SSD_INNER + SSD_INNER + 2 * SSD_BC
_IN_DT1 = _IN_DT0 + SSD_HEADS
_IN_END = _IN_DT1 + 3 * SB_WIDTH + 3 * len(DIL_PAIRS) * DIL_WIDTH + N_BRANCH * D_MODEL
COL_Z = 0
COL_X = SSD_INNER
COL_B = COL_X + SSD_INNER
COL_C = COL_B + SSD_BC
N_PROJ_A = COL_C + SSD_BC
WB_SB = 0
WB_DIL = WB_SB + 3 * SB_WIDTH
WB_GATE = WB_DIL + 3 * len(DIL_PAIRS) * DIL_WIDTH
WB_END = WB_GATE + N_BRANCH * D_MODEL
PB_SB = 0
PB_DIL = PB_SB + 3 * SB_WIDTH
PB_GATE = PB_DIL + 3 * DIL_WIDTH
N_PROJ_B = PB_GATE + N_BRANCH * D_MODEL
DT_LANES = 128
DT_WIDTH = SSD_GROUPS * DT_LANES

V7X_LANES = 128
V7X_VMEM_LIMIT = 48 * 1024 * 1024
MOE_VMEM_LIMIT = 56 * 1024 * 1024
MM_TM, MM_TN = 1024, 1024
CAST_TK = 1024
CAST_SPILL = 32
MERGE_TM, MERGE_TN = 512, 512
ROW_TM = 256
DIL_BLOCKS_PER_STEP = 4
SB_BLOCK = 256
SB_HEADS_PER_STEP = 4
SB_EXP2_CLAMP = 64.0
MOE_TM = 256


def _params(*sem):
    return pltpu.CompilerParams(dimension_semantics=sem, vmem_limit_bytes=V7X_VMEM_LIMIT)


def _dot(a, b):
    return jnp.dot(a, b, preferred_element_type=F32)


def _dot_nt(a, b):
    return lax.dot_general(a, b, (((1,), (1,)), ((), ())), preferred_element_type=F32)


def _split_dot(x, m, passes):
    acc = None
    r = x
    for p in range(passes):
        s = r.astype(BF16)
        d = _dot(s, m)
        acc = d if acc is None else acc + d
        if p + 1 < passes:
            r = r - s.astype(F32)
    return acc


def _split_dot_left(m, x, passes):
    acc = None
    r = x
    for p in range(passes):
        s = r.astype(BF16)
        d = _dot(m, s)
        acc = d if acc is None else acc + d
        if p + 1 < passes:
            r = r - s.astype(F32)
    return acc


def _softplus(x):
    return jnp.maximum(x, 0.0) + jnp.log1p(jnp.exp(-jnp.abs(x)))


def _silu(x):
    return x * jax.nn.sigmoid(x)


def _layer_norm(y, g, b):
    mu = jnp.mean(y, axis=-1, keepdims=True)
    yc = y - mu
    var = jnp.mean(yc * yc, axis=-1, keepdims=True)
    return yc * lax.rsqrt(var + EPS) * g + b


def _pack_halves(x):
    n = x.shape[1] // 2
    bits = pltpu.bitcast(x.astype(BF16).astype(F32), jnp.uint32)
    return (bits[:, n:] & jnp.uint32(0xFFFF0000)) | (bits[:, :n] >> 16)


def _unpack_halves(p):
    lo = pltpu.bitcast(p << 16, F32)
    hi = pltpu.bitcast(p & jnp.uint32(0xFFFF0000), F32)
    return jnp.concatenate([lo, hi], axis=1)


def _start_row_gather(idx_ref, base, src_hbm, dst, sem):
    def issue(r, carry):
        pltpu.make_async_copy(src_hbm.at[pl.ds(idx_ref[base + r], 1)], dst.at[pl.ds(r, 1)], sem).start()
        return carry

    lax.fori_loop(0, dst.shape[0], issue, 0, unroll=8)


def _wait_row_gather(src_hbm, dst, sem):
    pltpu.make_async_copy(src_hbm.at[pl.ds(0, dst.shape[0])], dst, sem).wait()


def _expander(rows, cols, width):
    h = lax.broadcasted_iota(jnp.int32, (rows, cols), 0)
    j = lax.broadcasted_iota(jnp.int32, (rows, cols), 1)
    return jnp.where(h == j // width, 1.0, 0.0).astype(BF16)


def _mm_kernel(a_ref, w_ref, o_ref):
    o_ref[...] = _dot(a_ref[...], w_ref[...]).astype(o_ref.dtype)


def _matmul(a, w, out_dtype, tm, tn, col_ranges=None, layer=None):
    m, k = a.shape
    tm, tn = min(tm, m), min(tn, w.shape[-1])
    col_ranges = col_ranges or ((0, w.shape[-1]),)
    assert all(c0 % tn == 0 and c1 % tn == 0 for c0, c1 in col_ranges)
    n = sum(c1 - c0 for c0, c1 in col_ranges)

    def w_block(j):
        blk, first = None, 0
        for c0, c1 in col_ranges:
            here = c0 // tn + (j - first)
            blk = here if blk is None else jnp.where(j >= first, here, blk)
            first += (c1 - c0) // tn
        return blk

    return pl.pallas_call(
        _mm_kernel,
        out_shape=jax.ShapeDtypeStruct((m, n), out_dtype),
        grid=(m // tm, n // tn),
        in_specs=[pl.BlockSpec((tm, k), lambda i, j: (i, 0)),
                  pl.BlockSpec((k, tn), lambda i, j: (0, w_block(j))) if layer is None else
                  pl.BlockSpec((None, k, tn), lambda i, j: (layer, 0, w_block(j)))],
        out_specs=pl.BlockSpec((tm, tn), lambda i, j: (i, j)),
        compiler_params=_params("parallel", "parallel"),
        name="matmul",
    )(a, w)


def _cast_cols_kernel(a_ref, b_ref, o_ref, *, shift):
    a = a_ref[0]
    if shift:
        a = jnp.concatenate([a, b_ref[0]], axis=0)[shift:shift + a.shape[0], :]
    o_ref[0] = a.T.astype(o_ref.dtype)


def _cast_cols(w_t, col0, n_out):
    depth, n_in, k = w_t.shape
    tk = CAST_TK
    tn = min(MM_TN, n_out)
    shift = col0 % tn
    base = col0 - shift
    assert shift <= CAST_SPILL and (base + tn) % CAST_SPILL == 0 and tn % CAST_SPILL == 0
    assert n_out % tn == 0 and k % tk == 0 and col0 + n_out <= n_in
    return pl.pallas_call(
        functools.partial(_cast_cols_kernel, shift=shift),
        out_shape=jax.ShapeDtypeStruct((depth, k, n_out), BF16),
        grid=(depth, k // tk, n_out // tn),
        in_specs=[pl.BlockSpec((1, tn, tk), lambda l, i, j: (l, base // tn + j, i)),
                  pl.BlockSpec((1, CAST_SPILL, tk), lambda l, i, j: (l, (base + tn * (j + 1)) // CAST_SPILL, i))],
        out_specs=pl.BlockSpec((1, tk, tn), lambda l, i, j: (l, i, j)),
        compiler_params=_params("parallel", "parallel", "parallel"),
        name="cast_cols",
    )(w_t, w_t)


def _ssd_kernel(z_ref, x_ref, b_ref, c_ref, dt_ref, cwx_ref, cwb_ref, cwc_ref, cbx_ref, cbb_ref, cbc_ref,
                dtb_ref, alog_ref, dskip_ref, g_ref, o_ref,
                state_ref, xpad_ref, bpad_ref, cpad_ref, xs_ref, bs_ref, cs_ref, y_ref,
                dts_ref, acs_ref, ecs_ref, dte_ref):
    chunk = x_ref.shape[1]
    pad = 8

    @pl.when(pl.program_id(1) == 0)
    def _():
        state_ref[...] = jnp.zeros_like(state_ref)
        xpad_ref[0:pad, :] = jnp.zeros((pad, xpad_ref.shape[1]), F32)
        bpad_ref[0:pad, :] = jnp.zeros((pad, bpad_ref.shape[1]), F32)
        cpad_ref[0:pad, :] = jnp.zeros((pad, cpad_ref.shape[1]), F32)

    def conv_silu(src_ref, pad_ref, cw_ref, cb_ref, dst_ref):
        pad_ref[pad:pad + chunk, :] = src_ref[0].astype(F32)
        acc = cb_ref[...]
        for k in range(SSD_CONV):
            off = pad - (SSD_CONV - 1) + k
            acc = acc + cw_ref[k:k + 1, :] * pad_ref[off:off + chunk, :]
        dst_ref[...] = _silu(acc)
        pad_ref[0:pad, :] = pad_ref[chunk:chunk + pad, :]

    conv_silu(x_ref, xpad_ref, cwx_ref, cbx_ref, xs_ref)
    conv_silu(b_ref, bpad_ref, cwb_ref, cbb_ref, bs_ref)
    conv_silu(c_ref, cpad_ref, cwc_ref, cbc_ref, cs_ref)

    dt = _softplus(dt_ref[0] + dtb_ref[...])
    a_dt = dt * (-jnp.exp(alog_ref[...]))
    row = lax.broadcasted_iota(jnp.int32, (chunk, chunk), 0)
    col = lax.broadcasted_iota(jnp.int32, (chunk, chunk), 1)
    causal = col <= row
    tri = jnp.where(causal, 1.0, 0.0).astype(BF16)
    a_cs = _split_dot_left(tri, a_dt, 3)
    last = a_cs[chunk - 1:chunk, :]
    dts_ref[...] = dt
    acs_ref[...] = a_cs
    ecs_ref[...] = jnp.exp(a_cs)
    dte_ref[...] = jnp.exp(last - a_cs)

    expand = _expander(DT_LANES, SSD_GROUP_WIDTH, SSD_HEAD_DIM)
    lane = lax.broadcasted_iota(jnp.int32, (chunk, V7X_LANES), 1)
    heads_per_tile = V7X_LANES // SSD_HEAD_DIM

    def group(g, carry):
        ch0 = pl.multiple_of(g * SSD_GROUP_WIDTH, SSD_GROUP_WIDTH)
        st0 = pl.multiple_of(g * SSD_STATE, SSD_STATE)
        dt0 = pl.multiple_of(g * DT_LANES, DT_LANES)
        xg = xs_ref[:, pl.ds(ch0, SSD_GROUP_WIDTH)]
        bg = bs_ref[:, pl.ds(st0, SSD_STATE)]
        cg = cs_ref[:, pl.ds(st0, SSD_STATE)].astype(BF16)
        csg = acs_ref[:, pl.ds(dt0, DT_LANES)]
        dt_x = _split_dot(dts_ref[:, pl.ds(dt0, DT_LANES)], expand, 2)
        ecs_x = _split_dot(ecs_ref[:, pl.ds(dt0, DT_LANES)], expand, 2)
        dte_x = _split_dot(dte_ref[:, pl.ds(dt0, DT_LANES)], expand, 2)
        xdt = xg * dt_x
        xdt_b = xdt.astype(BF16)
        cb = _dot_nt(cg, bg.astype(BF16))
        cs_t = csg.T
        prev = state_ref[g]
        y_off = _dot(cg, prev.astype(BF16)) * ecs_x
        for t in range(SSD_GROUP_WIDTH // V7X_LANES):
            lanes = slice(t * V7X_LANES, (t + 1) * V7X_LANES)
            xdt_t = xdt_b[:, lanes]
            y_t = None
            for u in range(heads_per_tile):
                r = t * heads_per_tile + u
                seg = csg[:, r:r + 1] - cs_t[r:r + 1, :]
                m = (cb * jnp.exp(jnp.where(causal, seg, NEG))).astype(BF16)
                y_r = _dot(m, xdt_t)
                y_t = y_r if y_t is None else jnp.where(lane // SSD_HEAD_DIM == u, y_r, y_t)
            c0 = pl.multiple_of(ch0 + t * V7X_LANES, V7X_LANES)
            y_ref[:, pl.ds(c0, V7X_LANES)] = (y_t + y_off[:, lanes]
                                              + dskip_ref[:, pl.ds(c0, V7X_LANES)] * xg[:, lanes])
        new = ecs_x[chunk - 1:chunk, :] * prev + _dot(bg.T.astype(BF16), (xdt * dte_x).astype(BF16))
        state_ref[g] = new
        return carry

    lax.fori_loop(0, SSD_GROUPS, group, 0)

    yf = y_ref[...] * _silu(z_ref[0].astype(F32))
    ms = jnp.mean(yf * yf, axis=-1, keepdims=True)
    o_ref[0] = (yf * lax.rsqrt(ms + EPS) * g_ref[...]).astype(o_ref.dtype)


def _ssd(proj, dt_raw, conv_w, conv_b, dt_bias, a_log, d_skip, norm_g):
    bsz, seq, _ = proj.shape
    chunk = min(SSD_CHUNK, seq)
    nc = seq // chunk

    def spread_heads(v):
        v = v.astype(F32).reshape(SSD_GROUPS, SSD_HEADS_PER_GROUP)
        return jnp.pad(v, ((0, 0), (0, DT_LANES - SSD_HEADS_PER_GROUP))).reshape(1, DT_WIDTH)

    cw = conv_w.astype(F32)
    cbias = conv_b.astype(F32).reshape(1, -1)
    x1, b1 = SSD_INNER, SSD_INNER + SSD_BC
    consts = [cw[:, :x1], cw[:, x1:b1], cw[:, b1:], cbias[:, :x1], cbias[:, x1:b1], cbias[:, b1:],
              spread_heads(dt_bias), spread_heads(a_log),
              jnp.repeat(d_skip.astype(F32), SSD_HEAD_DIM).reshape(1, SSD_INNER),
              norm_g.astype(F32).reshape(1, SSD_INNER)]

    def col_block(width, col):
        return pl.BlockSpec((1, chunk, width), lambda b, c: (b, c, col // width))

    def whole(a):
        return pl.BlockSpec(a.shape, lambda b, c: (0, 0))

    return pl.pallas_call(
        _ssd_kernel,
        out_shape=jax.ShapeDtypeStruct((bsz, seq, SSD_INNER), BF16),
        grid=(bsz, nc),
        in_specs=[col_block(SSD_INNER, COL_Z), col_block(SSD_INNER, COL_X),
                  col_block(SSD_BC, COL_B), col_block(SSD_BC, COL_C),
                  pl.BlockSpec((1, chunk, DT_WIDTH), lambda b, c: (b, c, 0))] + [whole(a) for a in consts],
        out_specs=pl.BlockSpec((1, chunk, SSD_INNER), lambda b, c: (b, c, 0)),
        scratch_shapes=[pltpu.VMEM((SSD_GROUPS, SSD_STATE, SSD_GROUP_WIDTH), F32),
                        pltpu.VMEM((chunk + 8, SSD_INNER), F32),
                        pltpu.VMEM((chunk + 8, SSD_BC), F32),
                        pltpu.VMEM((chunk + 8, SSD_BC), F32),
                        pltpu.VMEM((chunk, SSD_INNER), F32),
                        pltpu.VMEM((chunk, SSD_BC), F32),
                        pltpu.VMEM((chunk, SSD_BC), F32),
                        pltpu.VMEM((chunk, SSD_INNER), F32),
                        pltpu.VMEM((chunk, DT_WIDTH), F32),
                        pltpu.VMEM((chunk, DT_WIDTH), F32),
                        pltpu.VMEM((chunk, DT_WIDTH), F32),
                        pltpu.VMEM((chunk, DT_WIDTH), F32)],
        compiler_params=_params("parallel", "arbitrary"),
        name="ssd",
    )(proj, proj, proj, proj, dt_raw, *consts)


def _sb_kernel(q_ref, k_ref, v_ref, o_ref, *, blk, scale2):
    qi = pl.program_id(2)
    hd = SB_HEAD_DIM
    row = lax.broadcasted_iota(jnp.int32, (blk, blk), 0)
    col = lax.broadcasted_iota(jnp.int32, (blk, blk), 1)
    strict = col < row
    upper = jnp.where(row >= col, 1.0, 0.0).astype(BF16)
    upper2 = jnp.concatenate([upper, upper], axis=0)

    lanes = [slice(h * hd, (h + 1) * hd) for h in range(q_ref.shape[2] // hd)]

    def blocks(kb, rs, diagonal):
        start = pl.multiple_of(kb * blk, blk)
        zs = [_dot_nt(q_ref[0, :, sl], k_ref[0, pl.ds(start, blk), sl]) * scale2 for sl in lanes]
        sufs = []
        for z in zs:
            sp = jnp.maximum(z, jnp.log2(1.0 + jnp.exp2(jnp.minimum(z, SB_EXP2_CLAMP))))
            if diagonal:
                sp = jnp.where(strict, sp, 0.0)
            hi = pltpu.bitcast(pltpu.bitcast(sp, jnp.uint32) & jnp.uint32(0xFFFF0000), F32)
            parts = jnp.concatenate([hi.astype(BF16), (sp - hi).astype(BF16)], axis=1)
            sufs.append(_dot(parts, upper2))
        new_rs = tuple(r - suf[:, 0:1] for r, suf in zip(rs, sufs))
        pvs = []
        for z, suf, r, sl in zip(zs, sufs, rs, lanes):
            w = jnp.exp2(z - suf + r)
            if diagonal:
                w = jnp.where(strict, w, 0.0)
            pvs.append(_dot(w.astype(BF16), v_ref[0, pl.ds(start, blk), sl]))
        return new_rs, pvs

    def largest(rs):
        m = rs[0]
        for r in rs[1:]:
            m = jnp.maximum(m, r)
        return jnp.max(m)

    rs, accs = blocks(qi, tuple(jnp.zeros((blk, 1), F32) for _ in lanes), True)

    def more(c):
        return (c[0] >= 0) & (c[1] > F32_SUBNORMAL_EXP2)

    def body(c):
        kb, _, rs, accs = c
        rs, pvs = blocks(kb, rs, False)
        return kb - 1, largest(rs), rs, tuple(a + pv for a, pv in zip(accs, pvs))

    _, _, _, accs = lax.while_loop(more, body, (qi - 1, largest(rs), rs, tuple(accs)))
    o_ref[0] = jnp.concatenate(accs, axis=1).astype(o_ref.dtype)


def _stick_breaking(proj):
    bsz, seq, _ = proj.shape
    blk = min(SB_BLOCK, seq)
    qc, kc, vc = (PB_SB + i * SB_WIDTH for i in range(3))
    w = SB_HEADS_PER_STEP * SB_HEAD_DIM
    return pl.pallas_call(
        functools.partial(_sb_kernel, blk=blk, scale2=SB_HEAD_DIM ** -0.5 * LOG2E),
        out_shape=jax.ShapeDtypeStruct((bsz, seq, SB_WIDTH), BF16),
        grid=(bsz, SB_WIDTH // w, seq // blk),
        in_specs=[pl.BlockSpec((1, blk, w), lambda b, h, i: (b, i, qc // w + h)),
                  pl.BlockSpec((1, seq, w), lambda b, h, i: (b, 0, kc // w + h)),
                  pl.BlockSpec((1, seq, w), lambda b, h, i: (b, 0, vc // w + h))],
        out_specs=pl.BlockSpec((1, blk, w), lambda b, h, i: (b, i, h)),
        compiler_params=_params("parallel", "parallel", "arbitrary"),
        name="stick_breaking",
    )(proj, proj, proj)


def _dil_kernel(q_ref, kp_ref, kc_ref, vp_ref, vc_ref, o_ref, lse_ref, *, scale):
    bq = DIL_BLOCK
    nsub = q_ref.shape[1] // bq
    i = lax.broadcasted_iota(jnp.int32, (bq, 2 * bq), 0)
    m = lax.broadcasted_iota(jnp.int32, (bq, 2 * bq), 1)
    dist = i - m + bq
    band = (dist >= 0) & (dist <= bq)
    no_prev = jnp.where(pl.program_id(1) > 0, 0, bq)
    lane = lax.broadcasted_iota(jnp.int32, (bq, V7X_LANES), 1)
    for sub in range(nsub):
        rows = slice(sub * bq, (sub + 1) * bq)
        before = slice((sub - 1) * bq, sub * bq)
        valid = band & (m >= no_prev) if sub == 0 else band
        lse_all = jnp.zeros((bq, V7X_LANES), F32)
        for h in range(DIL_HEADS):
            sl = slice(h * DIL_HEAD_DIM, (h + 1) * DIL_HEAD_DIM)
            k_prev = kp_ref[0, :, sl] if sub == 0 else kc_ref[0, before, sl]
            v_prev = vp_ref[0, :, sl] if sub == 0 else vc_ref[0, before, sl]
            k = jnp.concatenate([k_prev, kc_ref[0, rows, sl]], axis=0)
            v = jnp.concatenate([v_prev, vc_ref[0, rows, sl]], axis=0)
            s = jnp.where(valid, _dot_nt(q_ref[0, rows, sl], k) * scale, NEG)
            mx = jnp.max(s, axis=-1, keepdims=True)
            p = jnp.exp(s - mx)
            den = jnp.sum(p, axis=-1, keepdims=True)
            o_ref[0, rows, sl] = (_dot(p.astype(BF16), v) / den).astype(o_ref.dtype)
            lse_all = jnp.where(lane == h, mx + jnp.log(den), lse_all)
        lse_ref[0, rows, :] = lse_all


def _band_attention(qkv, col):
    nz, lc, _ = qkv.shape
    bq = DIL_BLOCK
    w = DIL_WIDTH
    assert lc % bq == 0 and col % w == 0
    nsub = DIL_BLOCKS_PER_STEP if lc % (DIL_BLOCKS_PER_STEP * bq) == 0 else 1
    rows = nsub * bq

    def cur(part):
        return pl.BlockSpec((1, rows, w), lambda z, i: (z, i, col // w + part))

    def prev(part):
        return pl.BlockSpec((1, bq, w), lambda z, i: (z, jnp.maximum(i * nsub - 1, 0), col // w + part))

    return pl.pallas_call(
        functools.partial(_dil_kernel, scale=DIL_HEAD_DIM ** -0.5),
        out_shape=(jax.ShapeDtypeStruct((nz, lc, w), BF16),
                   jax.ShapeDtypeStruct((nz, lc, V7X_LANES), F32)),
        grid=(nz, lc // rows),
        in_specs=[cur(0), prev(1), cur(1), prev(2), cur(2)],
        out_specs=(pl.BlockSpec((1, rows, w), lambda z, i: (z, i, 0)),
                   pl.BlockSpec((1, rows, V7X_LANES), lambda z, i: (z, i, 0))),
        compiler_params=_params("parallel", "arbitrary"),
        name="dilated_attention",
    )(qkv, qkv, qkv, qkv, qkv)


def _residue_block(tm, seq, dilation, width):
    tiles = seq // tm
    return pl.BlockSpec((1, dilation, tm // dilation, width), lambda i, *_: (i // tiles, 0, i % tiles, 0))


def _emit_by_residue(x, strip_ref, dst_refs):
    tm, w = x.shape
    strips = w // V7X_LANES
    for s in range(strips):
        strip_ref[s] = x[:, s * V7X_LANES:(s + 1) * V7X_LANES]
    for dst in dst_refs:
        d, n = dst.shape[1], dst.shape[2]
        for c in range(d):
            for s in range(strips):
                dst[0, c, :, s * V7X_LANES:(s + 1) * V7X_LANES] = strip_ref[s, pl.ds(c, n, stride=d), :].astype(dst.dtype)


def _dilmix_kernel(o0_ref, l0_ref, o1_ref, l1_ref, o2_ref, l2_ref, out_ref, strip_ref, lse_ref):
    tm = out_ref.shape[0]
    heads = DIL_WIDTH // DIL_HEAD_DIM
    for gi, (o_ref, l_ref) in enumerate(((o1_ref, l1_ref), (o2_ref, l2_ref))):
        d = o_ref.shape[1]
        n = tm // d
        for c in range(d):
            lse_ref[gi, pl.ds(c, n, stride=d), :] = l_ref[0, c]
            for j in range(heads):
                strip_ref[gi, j, pl.ds(c, n, stride=d), :] = (
                    o_ref[0, c, :, j * DIL_HEAD_DIM:(j + 1) * DIL_HEAD_DIM].astype(F32))
    ls = [l0_ref[...], lse_ref[0], lse_ref[1]]
    mx = jnp.maximum(jnp.maximum(ls[0], ls[1]), ls[2])
    es = [jnp.exp(l - mx) for l in ls]
    inv = 1.0 / (es[0] + es[1] + es[2])
    ws = [e * inv for e in es]
    for j in range(heads):
        sl = slice(j * DIL_HEAD_DIM, (j + 1) * DIL_HEAD_DIM)
        acc = (ws[0][:, j:j + 1] * o0_ref[:, sl].astype(F32) + ws[1][:, j:j + 1] * strip_ref[0, j]
               + ws[2][:, j:j + 1] * strip_ref[1, j])
        out_ref[:, sl] = acc.astype(out_ref.dtype)


def _dilated_mixture(o0, l0, o1, l1, o2, l2, seq):
    t = o0.shape[0]
    tm = min(ROW_TM, t)
    heads = DIL_WIDTH // DIL_HEAD_DIM
    assert DIL_HEAD_DIM == V7X_LANES
    o_spec = pl.BlockSpec((tm, DIL_WIDTH), lambda i: (i, 0))
    l_spec = pl.BlockSpec((tm, V7X_LANES), lambda i: (i, 0))
    res = [spec for o in (o1, o2) for spec in (_residue_block(tm, seq, o.shape[1], DIL_WIDTH),
                                               _residue_block(tm, seq, o.shape[1], V7X_LANES))]
    return pl.pallas_call(
        _dilmix_kernel,
        out_shape=jax.ShapeDtypeStruct((t, DIL_WIDTH), BF16),
        grid=(t // tm,),
        in_specs=[o_spec, l_spec] + res,
        out_specs=o_spec,
        scratch_shapes=[pltpu.VMEM((2, heads, tm, DIL_HEAD_DIM), F32), pltpu.VMEM((2, tm, V7X_LANES), F32)],
        compiler_params=_params("parallel"),
        name="dilated_mix",
    )(o0, l0, o1, l1, o2, l2)


def _merge_kernel(ya_ref, yb_ref, yc_ref, ua_ref, ub_ref, uc_ref, ga_ref, gb_ref, gc_ref, bg_ref, o_ref):
    acc = None
    for i, (y_ref, u_ref, g_ref) in enumerate(((ya_ref, ua_ref, ga_ref), (yb_ref, ub_ref, gb_ref),
                                               (yc_ref, uc_ref, gc_ref))):
        gate = jax.nn.sigmoid(g_ref[...].astype(F32) + bg_ref[i:i + 1, :])
        term = gate * _dot(y_ref[...], u_ref[...])
        acc = term if acc is None else acc + term
    o_ref[...] = acc.astype(o_ref.dtype)


def _merge_up(y_ssd, y_sb, y_dil, layer, u_ssd, u_sb, u_dil, proj2d, b_gate):
    t = y_ssd.shape[0]
    tm, tn = min(MERGE_TM, t), MERGE_TN

    def rows(a):
        return pl.BlockSpec((tm, a.shape[1]), lambda i, j: (i, 0))

    def cols(a):
        return pl.BlockSpec((None, a.shape[1], tn), lambda i, j: (layer, 0, j))

    def gate(k):
        return pl.BlockSpec((tm, tn), lambda i, j: (i, (PB_GATE + k * D_MODEL) // tn + j))

    return pl.pallas_call(
        _merge_kernel,
        out_shape=jax.ShapeDtypeStruct((t, D_MODEL), BF16),
        grid=(t // tm, D_MODEL // tn),
        in_specs=[rows(y_ssd), rows(y_sb), rows(y_dil), cols(u_ssd), cols(u_sb), cols(u_dil),
                  gate(0), gate(1), gate(2), pl.BlockSpec((N_BRANCH, tn), lambda i, j: (0, j))],
        out_specs=pl.BlockSpec((tm, tn), lambda i, j: (i, j)),
        compiler_params=_params("parallel", "parallel"),
        name="merge_up",
    )(y_ssd, y_sb, y_dil, u_ssd, u_sb, u_dil, proj2d, proj2d, proj2d, b_gate)


def _outln_kernel(m_ref, w_ref, h_ref, g_ref, b_ref, rwh_ref, rwl_ref, rb_ref, of_ref, op_ref, meta_ref, wts_ref,
                  cnt_ref, *, alpha):
    @pl.when(pl.program_id(0) == 0)
    def _():
        cnt_ref[...] = jnp.zeros_like(cnt_ref)

    y = alpha * h_ref[...] + _dot(m_ref[...], w_ref[...])
    out = _layer_norm(y, g_ref[...], b_ref[...])
    of_ref[...] = out
    op_ref[...] = _pack_halves(out)
    e1, e2, w1, w2 = _route(out, rwh_ref[...], rwl_ref[...], rb_ref[...])
    r1, r2 = _expert_ranks(e1, e2, cnt_ref)
    for row, v in enumerate((e1, e2, r1, r2)):
        meta_ref[row:row + 1, :] = v
    wts_ref[0:1, :] = w1
    wts_ref[1:2, :] = w2


def _out_ln(merged, layer, w_out, h, g, b, router_w, router_bias, alpha):
    t, d = h.shape
    tm = min(ROW_TM, t)
    row = pl.BlockSpec((tm, d), lambda i: (i, 0))
    vec = pl.BlockSpec((1, d), lambda i: (0, 0))
    rw = jnp.pad(router_w.astype(F32), ((0, 0), (0, V7X_LANES - N_EXPERTS)))
    rw_hi = rw.astype(BF16)
    rw_lo = (rw - rw_hi.astype(F32)).astype(BF16)
    rw_spec = pl.BlockSpec((d, V7X_LANES), lambda i: (0, 0))
    return pl.pallas_call(
        functools.partial(_outln_kernel, alpha=alpha),
        out_shape=(jax.ShapeDtypeStruct((t, d), F32), jax.ShapeDtypeStruct((t, d // 2), jnp.uint32),
                   jax.ShapeDtypeStruct((4, t), jnp.int32), jax.ShapeDtypeStruct((2, t), F32),
                   jax.ShapeDtypeStruct((N_EXPERTS, V7X_LANES), F32)),
        grid=(t // tm,),
        in_specs=[row, pl.BlockSpec((None, d, d), lambda i: (layer, 0, 0)), row, vec, vec,
                  rw_spec, rw_spec, pl.BlockSpec((N_EXPERTS, 1), lambda i: (0, 0))],
        out_specs=(row, pl.BlockSpec((tm, d // 2), lambda i: (i, 0)),
                   pl.BlockSpec((4, tm), lambda i: (0, i)), pl.BlockSpec((2, tm), lambda i: (0, i)),
                   pl.BlockSpec((N_EXPERTS, V7X_LANES), lambda i: (0, 0))),
        compiler_params=_params("arbitrary"),
        name="out_ln",
    )(merged, w_out, h, g.reshape(1, d), b.reshape(1, d),
      rw_hi, rw_lo, router_bias.astype(F32).reshape(N_EXPERTS, 1))


def _route(h, rw_hi, rw_lo, rb):
    h_hi = h.astype(BF16)
    h_lo = (h - h_hi.astype(F32)).astype(BF16)
    logits = _dot(h_hi, rw_hi) + (_dot(h_lo, rw_hi) + _dot(h_hi, rw_lo))
    logits = logits.T[0:N_EXPERTS, :]
    mx = jnp.max(logits, axis=0, keepdims=True)
    ex = jnp.exp(logits - mx)
    aff = ex / jnp.sum(ex, axis=0, keepdims=True)
    sel = aff + rb
    s = [sel[e:e + 1, :] for e in range(N_EXPERTS)]
    a = [aff[e:e + 1, :] for e in range(N_EXPERTS)]
    n = EXPERTS_PER_GROUP
    best_score, best = None, None
    for g in range(N_EXPERT_GROUPS):
        grp = s[g * n:(g + 1) * n]
        score = None
        for i in range(n):
            for j in range(i + 1, n):
                pair = grp[i] + grp[j]
                score = pair if score is None else jnp.maximum(score, pair)
        if best is None:
            best_score, best = score, jnp.zeros_like(score, dtype=jnp.int32)
        else:
            take = score > best_score
            best_score = jnp.where(take, score, best_score)
            best = jnp.where(take, g, best)
    sv, av = [], []
    for r in range(n):
        sr, ar = s[r], a[r]
        for g in range(1, N_EXPERT_GROUPS):
            sr = jnp.where(best == g, s[g * n + r], sr)
            ar = jnp.where(best == g, a[g * n + r], ar)
        sv.append(sr)
        av.append(ar)

    def arg_top(vals, skip):
        top_v, top_i, top_a = None, None, None
        for r in range(n):
            v = vals[r] if skip is None else jnp.where(skip == r, -jnp.inf, vals[r])
            if top_v is None:
                top_v, top_i, top_a = v, jnp.zeros_like(best), av[0]
            else:
                take = v > top_v
                top_v = jnp.where(take, v, top_v)
                top_i = jnp.where(take, r, top_i)
                top_a = jnp.where(take, av[r], top_a)
        return top_i, top_a

    i1, a1 = arg_top(sv, None)
    i2, a2 = arg_top(sv, i1)
    tot = a1 + a2
    return best * n + i1, best * n + i2, a1 / tot, a2 / tot


def _expert_ranks(e1, e2, count_ref):
    tm = e1.shape[1]
    expert = lax.broadcasted_iota(jnp.int32, (N_EXPERTS, tm), 0)
    hit1, hit2 = expert == e1, expert == e2
    both = jnp.where(hit1, 1.0, 0.0) + jnp.where(hit2, 1.0, 0.0)
    earlier = lax.broadcasted_iota(jnp.int32, (tm, tm), 0) < lax.broadcasted_iota(jnp.int32, (tm, tm), 1)
    before = _dot(both.astype(BF16), jnp.where(earlier, 1.0, 0.0).astype(BF16)) + count_ref[:, 0:1]
    r1 = jnp.sum(jnp.where(hit1, before, 0.0), axis=0, keepdims=True)
    r2 = jnp.sum(jnp.where(hit2, before, 0.0), axis=0, keepdims=True)
    count_ref[...] = count_ref[...] + jnp.sum(both, axis=1, keepdims=True)
    return r1.astype(jnp.int32), r2.astype(jnp.int32)


def _moe_kernel(te_ref, nu_ref, fill_ref, next_ref, pos_ref, x_hbm, wg_hbm, wu_hbm, wd_hbm, o_ref,
                xbuf, sem, src_ref, stage_g, stage_u, stage_d, wbf_g, wbf_u, wbf_d, wsem, *, layer):
    i = pl.program_id(0)
    n_used = nu_ref[0]
    tm = xbuf.shape[1]
    n_tokens = pos_ref.shape[0] // 2
    w_hbm = (wg_hbm, wu_hbm, wd_hbm)
    stage = (stage_g, stage_u, stage_d)
    wbf = (wbf_g, wbf_u, wbf_d)

    def weight_copy(which, expert):
        return pltpu.make_async_copy(w_hbm[which].at[layer, expert], stage[which], wsem.at[which])

    @pl.when(i == 0)
    def _():
        for which in range(3):
            weight_copy(which, te_ref[0]).start()

    @pl.when((i < n_used) & ((i == 0) | (te_ref[jnp.maximum(i - 1, 0)] != te_ref[i])))
    def _():
        for which in range(3):
            weight_copy(which, te_ref[i]).wait()
            wbf[which][...] = stage[which][...].astype(BF16)

        @pl.when(next_ref[i] >= 0)
        def _():
            for which in range(3):
                weight_copy(which, next_ref[i]).start()

    @pl.when(i == 0)
    def _():
        def pad_expert(e, carry):
            def pad_slot(s, c):
                src_ref[s] = 0
                return c

            return lax.fori_loop(fill_ref[e], fill_ref[N_EXPERTS + e], pad_slot, carry)

        lax.fori_loop(0, N_EXPERTS, pad_expert, 0)
        for k in range(2):
            def place(tok, carry):
                src_ref[pos_ref[k * n_tokens + tok]] = tok
                return carry

            lax.fori_loop(0, n_tokens, place, 0, unroll=8)

    def fetch(tile):
        slot = tile % 2
        _start_row_gather(src_ref, tile * tm, x_hbm, xbuf.at[slot], sem.at[slot])

    @pl.when(i == 0)
    def _():
        fetch(i)

    @pl.when(i + 1 < n_used)
    def _():
        fetch(i + 1)

    @pl.when(i < n_used)
    def _():
        slot = i % 2
        _wait_row_gather(x_hbm, xbuf.at[slot], sem.at[slot])
        x = _unpack_halves(xbuf[slot]).astype(BF16)
        act = _silu(_dot(x, wbf_g[...])) * _dot(x, wbf_u[...])
        o_ref[...] = _pack_halves(_dot(act.astype(BF16), wbf_d[...]))

    @pl.when(i >= n_used)
    def _():
        o_ref[...] = jnp.zeros_like(o_ref)


def _route_plan(meta, counts, tm):
    t = meta.shape[1]
    padded = (counts + tm - 1) // tm * tm
    ends = jnp.cumsum(padded)
    starts = ends - padded
    one_hot = meta[0:2, :, None] == jnp.arange(N_EXPERTS, dtype=jnp.int32)
    pos = jnp.sum(jnp.where(one_hot, starts, 0), axis=-1) + meta[2:4]
    n_tiles = (2 * t + N_EXPERTS * tm) // tm
    n_used = (ends[-1] // tm).astype(jnp.int32)
    tile_start = jnp.minimum(jnp.arange(n_tiles, dtype=jnp.int32), n_used - 1) * tm
    owner = jnp.sum((ends[None, :] <= tile_start[:, None]).astype(jnp.int32), axis=1)
    owner = jnp.minimum(owner, N_EXPERTS - 1)
    fill = jnp.concatenate([starts + counts, ends]).astype(jnp.int32)
    expert = jnp.arange(N_EXPERTS, dtype=jnp.int32)
    later = (expert[None, :] > owner[:, None]) & (counts[None, :] > 0)
    next_expert = jnp.where(jnp.any(later, axis=1), jnp.argmax(later, axis=1).astype(jnp.int32), -1)
    return pos.reshape(-1), owner, n_used.reshape(1), fill, next_expert


def _moe_experts(x_packed, pos, tile_expert, n_used, fill, next_expert, layer, w_gate, w_up, w_down):
    n_tiles = tile_expert.shape[0]
    dp = x_packed.shape[1]
    tm = MOE_TM
    d, de = w_gate.shape[2], w_gate.shape[3]
    any_space = pl.BlockSpec(memory_space=pl.ANY)
    return pl.pallas_call(
        functools.partial(_moe_kernel, layer=layer),
        out_shape=jax.ShapeDtypeStruct((n_tiles * tm, dp), jnp.uint32),
        grid_spec=pltpu.PrefetchScalarGridSpec(
            num_scalar_prefetch=5,
            grid=(n_tiles,),
            in_specs=[any_space] * 4,
            out_specs=pl.BlockSpec((tm, dp), lambda i, *_: (i, 0)),
            scratch_shapes=[pltpu.VMEM((2, tm, dp), jnp.uint32), pltpu.SemaphoreType.DMA((2,)),
                            pltpu.SMEM((n_tiles * tm,), jnp.int32),
                            pltpu.VMEM((d, de), F32), pltpu.VMEM((d, de), F32), pltpu.VMEM((de, d), F32),
                            pltpu.VMEM((d, de), BF16), pltpu.VMEM((d, de), BF16), pltpu.VMEM((de, d), BF16),
                            pltpu.SemaphoreType.DMA((3,))]),
        compiler_params=pltpu.CompilerParams(dimension_semantics=("arbitrary",), vmem_limit_bytes=MOE_VMEM_LIMIT),
        name="moe_experts",
    )(tile_expert, n_used, fill, next_expert, pos, x_packed, w_gate, w_up, w_down)


def _ln2_kernel(pos_ref, h_ref, y_hbm, w_ref, g_ref, b_ref, of_ref, *rest, alpha, n_views):
    views, (rbuf, sem), strips = rest[:n_views], rest[n_views:n_views + 2], rest[n_views + 2:]
    i = pl.program_id(0)
    n_tiles = pl.num_programs(0)
    tm = h_ref.shape[0]

    def fetch(tile):
        slot = tile % 2
        for k in range(2):
            _start_row_gather(pos_ref, (k * n_tiles + tile) * tm, y_hbm, rbuf.at[slot, k], sem.at[slot, k])

    @pl.when(i == 0)
    def _():
        fetch(i)

    @pl.when(i + 1 < n_tiles)
    def _():
        fetch(i + 1)

    slot = i % 2
    y = alpha * h_ref[...]
    w = w_ref[...]
    for k in range(2):
        _wait_row_gather(y_hbm, rbuf.at[slot, k], sem.at[slot, k])
        y = y + w[:, k:k + 1] * _unpack_halves(rbuf[slot, k])
    out = _layer_norm(y, g_ref[...], b_ref[...])
    of_ref[...] = out
    if views:
        views[0][...] = out.astype(BF16)
        _emit_by_residue(out, strips[0], views[1:])


def _view_shapes_specs(bsz, seq, d, tm):
    shapes = [jax.ShapeDtypeStruct((bsz * seq, d), BF16)]
    specs = [pl.BlockSpec((tm, d), lambda i, *_: (i, 0))]
    for _, r in DIL_PAIRS[1:]:
        shapes.append(jax.ShapeDtypeStruct((bsz, r, seq // r, d), BF16))
        specs.append(_residue_block(tm, seq, r, d))
    return shapes, specs


def _combine_ln(h, y_packed, pos_by_choice, wts, g, b, alpha, bsz, seq, emit_views):
    t, d = h.shape
    dp = y_packed.shape[1]
    tm = min(ROW_TM, t)
    row = pl.BlockSpec((tm, d), lambda i, pos: (i, 0))
    vec = pl.BlockSpec((1, d), lambda i, pos: (0, 0))
    v_shapes, v_specs = _view_shapes_specs(bsz, seq, d, tm) if emit_views else ([], [])
    scratch = [pltpu.VMEM((2, 2, tm, dp), jnp.uint32), pltpu.SemaphoreType.DMA((2, 2))]
    if emit_views:
        scratch.append(pltpu.VMEM((d // V7X_LANES, tm, V7X_LANES), F32))
    out = pl.pallas_call(
        functools.partial(_ln2_kernel, alpha=alpha, n_views=len(v_shapes)),
        out_shape=[jax.ShapeDtypeStruct((t, d), F32)] + v_shapes,
        grid_spec=pltpu.PrefetchScalarGridSpec(
            num_scalar_prefetch=1,
            grid=(t // tm,),
            in_specs=[row, pl.BlockSpec(memory_space=pl.ANY), pl.BlockSpec((tm, 2), lambda i, pos: (i, 0)), vec, vec],
            out_specs=[row] + v_specs,
            scratch_shapes=scratch),
        compiler_params=_params("arbitrary"),
        name="combine_ln",
    )(pos_by_choice, h, y_packed, wts, g.reshape(1, d), b.reshape(1, d))
    return out[0], tuple(out[1:])


def _views_kernel(x_ref, *rest):
    views, strip_ref = rest[:-1], rest[-1]
    x = x_ref[...]
    views[0][...] = x.astype(BF16)
    _emit_by_residue(x, strip_ref, views[1:])


def _bf16_views(x2d, bsz, seq):
    t, d = x2d.shape
    tm = min(ROW_TM, t)
    v_shapes, v_specs = _view_shapes_specs(bsz, seq, d, tm)
    return tuple(pl.pallas_call(
        _views_kernel,
        out_shape=v_shapes,
        grid=(t // tm,),
        in_specs=[pl.BlockSpec((tm, d), lambda i: (i, 0))],
        out_specs=v_specs,
        scratch_shapes=[pltpu.VMEM((d // V7X_LANES, tm, V7X_LANES), F32)],
        compiler_params=_params("parallel"),
        name="bf16_views",
    )(x2d))


def _layer(h, views, bsz, seq, alpha, layer, last, w_a, w_b, w_dt, b_gate, conv_w, conv_b, dt_bias, a_log, d_skip,
           ssm_norm_g, w_up_ssd, w_up_sb, w_up_dil, w_out, ln1_g, ln1_b, router_w, router_bias,
           w_gate_e, w_up_e, w_down_e, ln2_g, ln2_b):
    t = bsz * seq
    h_b = views[0]
    dil_w = 3 * DIL_WIDTH
    w_dt = w_dt.reshape(D_MODEL, SSD_GROUPS, SSD_HEADS_PER_GROUP)
    w_dt = jnp.pad(w_dt, ((0, 0), (0, 0), (0, DT_LANES - SSD_HEADS_PER_GROUP))).reshape(D_MODEL, DT_WIDTH)
    proj_a = _matmul(h_b, w_a, BF16, MM_TM, MM_TN, layer=layer).reshape(bsz, seq, N_PROJ_A)
    proj2d = _matmul(h_b, w_b, BF16, MM_TM, MM_TN, layer=layer,
                     col_ranges=((WB_SB, WB_DIL + dil_w), (WB_GATE, WB_END)))
    dt_raw = _matmul(h_b, w_dt, F32, MM_TM, DT_WIDTH)
    proj = proj2d.reshape(bsz, seq, N_PROJ_B)

    y_ssd = _ssd(proj_a, dt_raw.reshape(bsz, seq, DT_WIDTH), conv_w, conv_b, dt_bias, a_log, d_skip, ssm_norm_g)
    y_sb = _stick_breaking(proj)

    assert DIL_PAIRS[0][1] == 1 and all(w // d == DIL_BLOCK for w, d in DIL_PAIRS)
    o, lse = _band_attention(proj, PB_DIL)
    dil = [o.reshape(t, DIL_WIDTH), lse.reshape(t, V7X_LANES)]
    for g, (_, r) in enumerate(DIL_PAIRS[1:], start=1):
        proj_g = _matmul(views[g].reshape(t, D_MODEL), w_b, BF16, MM_TM, MM_TN, layer=layer,
                         col_ranges=((WB_DIL + g * dil_w, WB_DIL + (g + 1) * dil_w),))
        o, lse = _band_attention(proj_g.reshape(bsz * r, seq // r, dil_w), 0)
        dil += [o.reshape(bsz, r, seq // r, DIL_WIDTH), lse.reshape(bsz, r, seq // r, V7X_LANES)]
    y_dil = _dilated_mixture(*dil, seq)

    merged = _merge_up(y_ssd.reshape(t, SSD_INNER), y_sb.reshape(t, SB_WIDTH), y_dil, layer,
                       w_up_ssd, w_up_sb, w_up_dil, proj2d, b_gate.astype(F32).reshape(N_BRANCH, D_MODEL))
    h1, h1_p, meta, wts, counts = _out_ln(merged, layer, w_out, h, ln1_g, ln1_b, router_w, router_bias, alpha)

    pos, tile_expert, n_used, fill, next_expert = _route_plan(meta, counts[:, 0].astype(jnp.int32), MOE_TM)
    y_p = _moe_experts(h1_p, pos, tile_expert, n_used, fill, next_expert, layer, w_gate_e, w_up_e, w_down_e)
    return _combine_ln(h1, y_p, pos, wts.T, ln2_g, ln2_b, alpha, bsz, seq, emit_views=not last)


def kernel(x, w_in, b_gate, conv_w, conv_b, dt_bias, a_log, d_skip, ssm_norm_g, w_up_ssd, w_up_sb, w_up_dil,
           w_out, ln1_g, ln1_b, router_w, router_bias, w_gate_e, w_up_e, w_down_e, ln2_g, ln2_b):
    bsz, seq, d = x.shape
    depth = w_in.shape[0]
    alpha = (2 * depth) ** 0.25
    h = x.reshape(bsz * seq, d)
    views = _bf16_views(h, bsz, seq)
    assert w_in.shape[2] == _IN_END
    w_t = jnp.swapaxes(w_in, 1, 2)
    w_a = _cast_cols(w_t, 0, N_PROJ_A)
    w_b = _cast_cols(w_t, _IN_DT1, WB_END)
    w_dt = _cast_cols(w_t, _IN_DT0, V7X_LANES)[:, :, :SSD_HEADS]
    w_up_ssd, w_up_sb, w_up_dil, w_out = (w.astype(BF16) for w in (w_up_ssd, w_up_sb, w_up_dil, w_out))
    for l in range(depth):
        h, views = _layer(h, views, bsz, seq, alpha, l, l == depth - 1, w_a, w_b, w_dt[l],
                          b_gate[l], conv_w[l], conv_b[l], dt_bias[l], a_log[l], d_skip[l], ssm_norm_g[l],
                          w_up_ssd, w_up_sb, w_up_dil, w_out, ln1_g[l], ln1_b[l], router_w, router_bias,
                          w_gate_e, w_up_e, w_down_e, ln2_g[l], ln2_b[l])
    return h.reshape(bsz, seq, d)
```

```python
import functools

import jax
import jax.numpy as jnp
from jax import lax
from jax.experimental import pallas as pl
from jax.experimental.pallas import tpu as pltpu

F32 = jnp.float32
BF16 = jnp.bfloat16

D_MODEL = 2048
SSD_HEADS = 32
SSD_HEAD_DIM = 64
SSD_INNER = SSD_HEADS * SSD_HEAD_DIM
SSD_STATE = 128
SSD_GROUPS = 4
SSD_HEADS_PER_GROUP = SSD_HEADS // SSD_GROUPS
SSD_GROUP_WIDTH = SSD_HEADS_PER_GROUP * SSD_HEAD_DIM
SSD_CONV = 4
SSD_CHUNK = 256
SSD_BC = SSD_GROUPS * SSD_STATE
SB_HEADS = 16
SB_HEAD_DIM = 128
SB_WIDTH = SB_HEADS * SB_HEAD_DIM
DIL_PAIRS = ((128, 1), (512, 4), (2048, 16))
DIL_HEADS = 8
DIL_HEAD_DIM = 128
DIL_WIDTH = DIL_HEADS * DIL_HEAD_DIM
DIL_BLOCK = 128
N_BRANCH = 3
N_EXPERTS = 16
EXPERTS_PER_GROUP = 4
N_EXPERT_GROUPS = N_EXPERTS // EXPERTS_PER_GROUP
D_EXPERT = 1024
EPS = 1e-5
NEG = -1e30
LOG2E = 1.4426950408889634
F32_SUBNORMAL_EXP2 = -128.0

_IN_DT0 = SSD_INNER + SSD_INNER + 2 * SSD_BC
_IN_DT1 = _IN_DT0 + SSD_HEADS
_IN_END = _IN_DT1 + 3 * SB_WIDTH + 3 * len(DIL_PAIRS) * DIL_WIDTH + N_BRANCH * D_MODEL
COL_Z = 0
COL_X = SSD_INNER
COL_B = COL_X + SSD_INNER
COL_C = COL_B + SSD_BC
N_PROJ_A = COL_C + SSD_BC
WB_SB = 0
WB_DIL = WB_SB + 3 * SB_WIDTH
WB_GATE = WB_DIL + 3 * len(DIL_PAIRS) * DIL_WIDTH
WB_END = WB_GATE + N_BRANCH * D_MODEL
PB_SB = 0
PB_DIL = PB_SB + 3 * SB_WIDTH
PB_GATE = PB_DIL + 3 * DIL_WIDTH
N_PROJ_B = PB_GATE + N_BRANCH * D_MODEL
DT_LANES = 128
DT_WIDTH = SSD_GROUPS * DT_LANES

V7X_LANES = 128
V7X_VMEM_LIMIT = 48 * 1024 * 1024
MOE_VMEM_LIMIT = 56 * 1024 * 1024
MM_TM, MM_TN = 1024, 1024
CAST_TK = 1024
CAST_SPILL = 32
MERGE_TM, MERGE_TN = 512, 512
ROW_TM = 256
DIL_BLOCKS_PER_STEP = 4
SB_BLOCK = 256
SB_HEADS_PER_STEP = 4
SB_EXP2_CLAMP = 64.0
MOE_TM = 256


def _params(*sem):
    return pltpu.CompilerParams(dimension_semantics=sem, vmem_limit_bytes=V7X_VMEM_LIMIT)


def _dot(a, b):
    return jnp.dot(a, b, preferred_element_type=F32)


def _dot_nt(a, b):
    return lax.dot_general(a, b, (((1,), (1,)), ((), ())), preferred_element_type=F32)


def _split_dot(x, m, passes):
    acc = None
    r = x
    for p in range(passes):
        s = r.astype(BF16)
        d = _dot(s, m)
        acc = d if acc is None else acc + d
        if p + 1 < passes:
            r = r - s.astype(F32)
    return acc


def _split_dot_left(m, x, passes):
    acc = None
    r = x
    for p in range(passes):
        s = r.astype(BF16)
        d = _dot(m, s)
        acc = d if acc is None else acc + d
        if p + 1 < passes:
            r = r - s.astype(F32)
    return acc


def _softplus(x):
    return jnp.maximum(x, 0.0) + jnp.log1p(jnp.exp(-jnp.abs(x)))


def _silu(x):
    return x * jax.nn.sigmoid(x)


def _layer_norm(y, g, b):
    mu = jnp.mean(y, axis=-1, keepdims=True)
    yc = y - mu
    var = jnp.mean(yc * yc, axis=-1, keepdims=True)
    return yc * lax.rsqrt(var + EPS) * g + b


def _pack_halves(x):
    n = x.shape[1] // 2
    bits = pltpu.bitcast(x.astype(BF16).astype(F32), jnp.uint32)
    return (bits[:, n:] & jnp.uint32(0xFFFF0000)) | (bits[:, :n] >> 16)


def _unpack_halves(p):
    lo = pltpu.bitcast(p << 16, F32)
    hi = pltpu.bitcast(p & jnp.uint32(0xFFFF0000), F32)
    return jnp.concatenate([lo, hi], axis=1)


def _start_row_gather(idx_ref, base, src_hbm, dst, sem):
    def issue(r, carry):
        pltpu.make_async_copy(src_hbm.at[pl.ds(idx_ref[base + r], 1)], dst.at[pl.ds(r, 1)], sem).start()
        return carry

    lax.fori_loop(0, dst.shape[0], issue, 0, unroll=8)


def _wait_row_gather(src_hbm, dst, sem):
    pltpu.make_async_copy(src_hbm.at[pl.ds(0, dst.shape[0])], dst, sem).wait()


def _expander(rows, cols, width):
    h = lax.broadcasted_iota(jnp.int32, (rows, cols), 0)
    j = lax.broadcasted_iota(jnp.int32, (rows, cols), 1)
    return jnp.where(h == j // width, 1.0, 0.0).astype(BF16)


def _mm_kernel(a_ref, w_ref, o_ref):
    o_ref[...] = _dot(a_ref[...], w_ref[...]).astype(o_ref.dtype)


def _matmul(a, w, out_dtype, tm, tn, col_ranges=None, layer=None):
    m, k = a.shape
    tm, tn = min(tm, m), min(tn, w.shape[-1])
    col_ranges = col_ranges or ((0, w.shape[-1]),)
    assert all(c0 % tn == 0 and c1 % tn == 0 for c0, c1 in col_ranges)
    n = sum(c1 - c0 for c0, c1 in col_ranges)

    def w_block(j):
        blk, first = None, 0
        for c0, c1 in col_ranges:
            here = c0 // tn + (j - first)
            blk = here if blk is None else jnp.where(j >= first, here, blk)
            first += (c1 - c0) // tn
        return blk

    return pl.pallas_call(
        _mm_kernel,
        out_shape=jax.ShapeDtypeStruct((m, n), out_dtype),
        grid=(m // tm, n // tn),
        in_specs=[pl.BlockSpec((tm, k), lambda i, j: (i, 0)),
                  pl.BlockSpec((k, tn), lambda i, j: (0, w_block(j))) if layer is None else
                  pl.BlockSpec((None, k, tn), lambda i, j: (layer, 0, w_block(j)))],
        out_specs=pl.BlockSpec((tm, tn), lambda i, j: (i, j)),
        compiler_params=_params("parallel", "parallel"),
        name="matmul",
    )(a, w)


def _cast_cols_kernel(a_ref, b_ref, o_ref, *, shift):
    a = a_ref[0]
    if shift:
        a = jnp.concatenate([a, b_ref[0]], axis=0)[shift:shift + a.shape[0], :]
    o_ref[0] = a.T.astype(o_ref.dtype)


def _cast_cols(w_t, col0, n_out):
    depth, n_in, k = w_t.shape
    tk = CAST_TK
    tn = min(MM_TN, n_out)
    shift = col0 % tn
    base = col0 - shift
    assert shift <= CAST_SPILL and (base + tn) % CAST_SPILL == 0 and tn % CAST_SPILL == 0
    assert n_out % tn == 0 and k % tk == 0 and col0 + n_out <= n_in
    return pl.pallas_call(
        functools.partial(_cast_cols_kernel, shift=shift),
        out_shape=jax.ShapeDtypeStruct((depth, k, n_out), BF16),
        grid=(depth, k // tk, n_out // tn),
        in_specs=[pl.BlockSpec((1, tn, tk), lambda l, i, j: (l, base // tn + j, i)),
                  pl.BlockSpec((1, CAST_SPILL, tk), lambda l, i, j: (l, (base + tn * (j + 1)) // CAST_SPILL, i))],
        out_specs=pl.BlockSpec((1, tk, tn), lambda l, i, j: (l, i, j)),
        compiler_params=_params("parallel", "parallel", "parallel"),
        name="cast_cols",
    )(w_t, w_t)


def _ssd_kernel(z_ref, x_ref, b_ref, c_ref, dt_ref, cwx_ref, cwb_ref, cwc_ref, cbx_ref, cbb_ref, cbc_ref,
                dtb_ref, alog_ref, dskip_ref, g_ref, o_ref,
                state_ref, xpad_ref, bpad_ref, cpad_ref, xs_ref, bs_ref, cs_ref, y_ref,
                dts_ref, acs_ref, ecs_ref, dte_ref):
    chunk = x_ref.shape[1]
    pad = 8

    @pl.when(pl.program_id(1) == 0)
    def _():
        state_ref[...] = jnp.zeros_like(state_ref)
        xpad_ref[0:pad, :] = jnp.zeros((pad, xpad_ref.shape[1]), F32)
        bpad_ref[0:pad, :] = jnp.zeros((pad, bpad_ref.shape[1]), F32)
        cpad_ref[0:pad, :] = jnp.zeros((pad, cpad_ref.shape[1]), F32)

    def conv_silu(src_ref, pad_ref, cw_ref, cb_ref, dst_ref):
        pad_ref[pad:pad + chunk, :] = src_ref[0].astype(F32)
        acc = cb_ref[...]
        for k in range(SSD_CONV):
            off = pad - (SSD_CONV - 1) + k
            acc = acc + cw_ref[k:k + 1, :] * pad_ref[off:off + chunk, :]
        dst_ref[...] = _silu(acc)
        pad_ref[0:pad, :] = pad_ref[chunk:chunk + pad, :]

    conv_silu(x_ref, xpad_ref, cwx_ref, cbx_ref, xs_ref)
    conv_silu(b_ref, bpad_ref, cwb_ref, cbb_ref, bs_ref)
    conv_silu(c_ref, cpad_ref, cwc_ref, cbc_ref, cs_ref)

    dt = _softplus(dt_ref[0] + dtb_ref[...])
    a_dt = dt * (-jnp.exp(alog_ref[...]))
    row = lax.broadcasted_iota(jnp.int32, (chunk, chunk), 0)
    col = lax.broadcasted_iota(jnp.int32, (chunk, chunk), 1)
    causal = col <= row
    tri = jnp.where(causal, 1.0, 0.0).astype(BF16)
    a_cs = _split_dot_left(tri, a_dt, 3)
    last = a_cs[chunk - 1:chunk, :]
    dts_ref[...] = dt
    acs_ref[...] = a_cs
    ecs_ref[...] = jnp.exp(a_cs)
    dte_ref[...] = jnp.exp(last - a_cs)

    expand = _expander(DT_LANES, SSD_GROUP_WIDTH, SSD_HEAD_DIM)
    lane = lax.broadcasted_iota(jnp.int32, (chunk, V7X_LANES), 1)
    heads_per_tile = V7X_LANES // SSD_HEAD_DIM

    def group(g, carry):
        ch0 = pl.multiple_of(g * SSD_GROUP_WIDTH, SSD_GROUP_WIDTH)
        st0 = pl.multiple_of(g * SSD_STATE, SSD_STATE)
        dt0 = pl.multiple_of(g * DT_LANES, DT_LANES)
        xg = xs_ref[:, pl.ds(ch0, SSD_GROUP_WIDTH)]
        bg = bs_ref[:, pl.ds(st0, SSD_STATE)]
        cg = cs_ref[:, pl.ds(st0, SSD_STATE)].astype(BF16)
        csg = acs_ref[:, pl.ds(dt0, DT_LANES)]
        dt_x = _split_dot(dts_ref[:, pl.ds(dt0, DT_LANES)], expand, 2)
        ecs_x = _split_dot(ecs_ref[:, pl.ds(dt0, DT_LANES)], expand, 2)
        dte_x = _split_dot(dte_ref[:, pl.ds(dt0, DT_LANES)], expand, 2)
        xdt = xg * dt_x
        xdt_b = xdt.astype(BF16)
        cb = _dot_nt(cg, bg.astype(BF16))
        cs_t = csg.T
        prev = state_ref[g]
        y_off = _dot(cg, prev.astype(BF16)) * ecs_x
        for t in range(SSD_GROUP_WIDTH // V7X_LANES):
            lanes = slice(t * V7X_LANES, (t + 1) * V7X_LANES)
            xdt_t = xdt_b[:, lanes]
            y_t = None
            for u in range(heads_per_tile):
                r = t * heads_per_tile + u
                seg = csg[:, r:r + 1] - cs_t[r:r + 1, :]
                m = (cb * jnp.exp(jnp.where(causal, seg, NEG))).astype(BF16)
                y_r = _dot(m, xdt_t)
                y_t = y_r if y_t is None else jnp.where(lane // SSD_HEAD_DIM == u, y_r, y_t)
            c0 = pl.multiple_of(ch0 + t * V7X_LANES, V7X_LANES)
            y_ref[:, pl.ds(c0, V7X_LANES)] = (y_t + y_off[:, lanes]
                                              + dskip_ref[:, pl.ds(c0, V7X_LANES)] * xg[:, lanes])
        new = ecs_x[chunk - 1:chunk, :] * prev + _dot(bg.T.astype(BF16), (xdt * dte_x).astype(BF16))
        state_ref[g] = new
        return carry

    lax.fori_loop(0, SSD_GROUPS, group, 0)

    yf = y_ref[...] * _silu(z_ref[0].astype(F32))
    ms = jnp.mean(yf * yf, axis=-1, keepdims=True)
    o_ref[0] = (yf * lax.rsqrt(ms + EPS) * g_ref[...]).astype(o_ref.dtype)


def _ssd(proj, dt_raw, conv_w, conv_b, dt_bias, a_log, d_skip, norm_g):
    bsz, seq, _ = proj.shape
    chunk = min(SSD_CHUNK, seq)
    nc = seq // chunk

    def spread_heads(v):
        v = v.astype(F32).reshape(SSD_GROUPS, SSD_HEADS_PER_GROUP)
        return jnp.pad(v, ((0, 0), (0, DT_LANES - SSD_HEADS_PER_GROUP))).reshape(1, DT_WIDTH)

    cw = conv_w.astype(F32)
    cbias = conv_b.astype(F32).reshape(1, -1)
    x1, b1 = SSD_INNER, SSD_INNER + SSD_BC
    consts = [cw[:, :x1], cw[:, x1:b1], cw[:, b1:], cbias[:, :x1], cbias[:, x1:b1], cbias[:, b1:],
              spread_heads(dt_bias), spread_heads(a_log),
              jnp.repeat(d_skip.astype(F32), SSD_HEAD_DIM).reshape(1, SSD_INNER),
              norm_g.astype(F32).reshape(1, SSD_INNER)]

    def col_block(width, col):
        return pl.BlockSpec((1, chunk, width), lambda b, c: (b, c, col // width))

    def whole(a):
        return pl.BlockSpec(a.shape, lambda b, c: (0, 0))

    return pl.pallas_call(
        _ssd_kernel,
        out_shape=jax.ShapeDtypeStruct((bsz, seq, SSD_INNER), BF16),
        grid=(bsz, nc),
        in_specs=[col_block(SSD_INNER, COL_Z), col_block(SSD_INNER, COL_X),
                  col_block(SSD_BC, COL_B), col_block(SSD_BC, COL_C),
                  pl.BlockSpec((1, chunk, DT_WIDTH), lambda b, c: (b, c, 0))] + [whole(a) for a in consts],
        out_specs=pl.BlockSpec((1, chunk, SSD_INNER), lambda b, c: (b, c, 0)),
        scratch_shapes=[pltpu.VMEM((SSD_GROUPS, SSD_STATE, SSD_GROUP_WIDTH), F32),
                        pltpu.VMEM((chunk + 8, SSD_INNER), F32),
                        pltpu.VMEM((chunk + 8, SSD_BC), F32),
                        pltpu.VMEM((chunk + 8, SSD_BC), F32),
                        pltpu.VMEM((chunk, SSD_INNER), F32),
                        pltpu.VMEM((chunk, SSD_BC), F32),
                        pltpu.VMEM((chunk, SSD_BC), F32),
                        pltpu.VMEM((chunk, SSD_INNER), F32),
                        pltpu.VMEM((chunk, DT_WIDTH), F32),
                        pltpu.VMEM((chunk, DT_WIDTH), F32),
                        pltpu.VMEM((chunk, DT_WIDTH), F32),
                        pltpu.VMEM((chunk, DT_WIDTH), F32)],
        compiler_params=_params("parallel", "arbitrary"),
        name="ssd",
    )(proj, proj, proj, proj, dt_raw, *consts)


def _sb_kernel(q_ref, k_ref, v_ref, o_ref, *, blk, scale2):
    qi = pl.program_id(2)
    hd = SB_HEAD_DIM
    row = lax.broadcasted_iota(jnp.int32, (blk, blk), 0)
    col = lax.broadcasted_iota(jnp.int32, (blk, blk), 1)
    strict = col < row
    upper = jnp.where(row >= col, 1.0, 0.0).astype(BF16)
    upper2 = jnp.concatenate([upper, upper], axis=0)

    lanes = [slice(h * hd, (h + 1) * hd) for h in range(q_ref.shape[2] // hd)]

    def blocks(kb, rs, diagonal):
        start = pl.multiple_of(kb * blk, blk)
        zs = [_dot_nt(q_ref[0, :, sl], k_ref[0, pl.ds(start, blk), sl]) * scale2 for sl in lanes]
        sufs = []
        for z in zs:
            sp = jnp.maximum(z, jnp.log2(1.0 + jnp.exp2(jnp.minimum(z, SB_EXP2_CLAMP))))
            if diagonal:
                sp = jnp.where(strict, sp, 0.0)
            hi = pltpu.bitcast(pltpu.bitcast(sp, jnp.uint32) & jnp.uint32(0xFFFF0000), F32)
            parts = jnp.concatenate([hi.astype(BF16), (sp - hi).astype(BF16)], axis=1)
            sufs.append(_dot(parts, upper2))
        new_rs = tuple(r - suf[:, 0:1] for r, suf in zip(rs, sufs))
        pvs = []
        for z, suf, r, sl in zip(zs, sufs, rs, lanes):
            w = jnp.exp2(z - suf + r)
            if diagonal:
                w = jnp.where(strict, w, 0.0)
            pvs.append(_dot(w.astype(BF16), v_ref[0, pl.ds(start, blk), sl]))
        return new_rs, pvs

    def largest(rs):
        m = rs[0]
        for r in rs[1:]:
            m = jnp.maximum(m, r)
        return jnp.max(m)

    rs, accs = blocks(qi, tuple(jnp.zeros((blk, 1), F32) for _ in lanes), True)

    def more(c):
        return (c[0] >= 0) & (c[1] > F32_SUBNORMAL_EXP2)

    def body(c):
        kb, _, rs, accs = c
        rs, pvs = blocks(kb, rs, False)
        return kb - 1, largest(rs), rs, tuple(a + pv for a, pv in zip(accs, pvs))

    _, _, _, accs = lax.while_loop(more, body, (qi - 1, largest(rs), rs, tuple(accs)))
    o_ref[0] = jnp.concatenate(accs, axis=1).astype(o_ref.dtype)


def _stick_breaking(proj):
    bsz, seq, _ = proj.shape
    blk = min(SB_BLOCK, seq)
    qc, kc, vc = (PB_SB + i * SB_WIDTH for i in range(3))
    w = SB_HEADS_PER_STEP * SB_HEAD_DIM
    return pl.pallas_call(
        functools.partial(_sb_kernel, blk=blk, scale2=SB_HEAD_DIM ** -0.5 * LOG2E),
        out_shape=jax.ShapeDtypeStruct((bsz, seq, SB_WIDTH), BF16),
        grid=(bsz, SB_WIDTH // w, seq // blk),
        in_specs=[pl.BlockSpec((1, blk, w), lambda b, h, i: (b, i, qc // w + h)),
                  pl.BlockSpec((1, seq, w), lambda b, h, i: (b, 0, kc // w + h)),
                  pl.BlockSpec((1, seq, w), lambda b, h, i: (b, 0, vc // w + h))],
        out_specs=pl.BlockSpec((1, blk, w), lambda b, h, i: (b, i, h)),
        compiler_params=_params("parallel", "parallel", "arbitrary"),
        name="stick_breaking",
    )(proj, proj, proj)


def _dil_kernel(q_ref, kp_ref, kc_ref, vp_ref, vc_ref, o_ref, lse_ref, *, scale):
    bq = DIL_BLOCK
    nsub = q_ref.shape[1] // bq
    i = lax.broadcasted_iota(jnp.int32, (bq, 2 * bq), 0)
    m = lax.broadcasted_iota(jnp.int32, (bq, 2 * bq), 1)
    dist = i - m + bq
    band = (dist >= 0) & (dist <= bq)
    no_prev = jnp.where(pl.program_id(1) > 0, 0, bq)
    lane = lax.broadcasted_iota(jnp.int32, (bq, V7X_LANES), 1)
    for z in range(q_ref.shape[0]):
        for sub in range(nsub):
            rows = slice(sub * bq, (sub + 1) * bq)
            before = slice((sub - 1) * bq, sub * bq)
            valid = band & (m >= no_prev) if sub == 0 else band
            lse_all = jnp.zeros((bq, V7X_LANES), F32)
            for h in range(DIL_HEADS):
                sl = slice(h * DIL_HEAD_DIM, (h + 1) * DIL_HEAD_DIM)
                k_prev = kp_ref[z, :, sl] if sub == 0 else kc_ref[z, before, sl]
                v_prev = vp_ref[z, :, sl] if sub == 0 else vc_ref[z, before, sl]
                k = jnp.concatenate([k_prev, kc_ref[z, rows, sl]], axis=0)
                v = jnp.concatenate([v_prev, vc_ref[z, rows, sl]], axis=0)
                s = jnp.where(valid, _dot_nt(q_ref[z, rows, sl], k) * scale, NEG)
                mx = jnp.max(s, axis=-1, keepdims=True)
                p = jnp.exp(s - mx)
                den = jnp.sum(p, axis=-1, keepdims=True)
                o_ref[z, rows, sl] = (_dot(p.astype(BF16), v) / den).astype(o_ref.dtype)
                lse_all = jnp.where(lane == h, mx + jnp.log(den), lse_all)
            lse_ref[z, rows, :] = lse_all


def _band_attention(qkv, col):
    nz, lc, _ = qkv.shape
    bq = DIL_BLOCK
    w = DIL_WIDTH
    assert lc % bq == 0 and col % w == 0
    nsub = DIL_BLOCKS_PER_STEP if lc % (DIL_BLOCKS_PER_STEP * bq) == 0 else 1
    zb = DIL_BLOCKS_PER_STEP // nsub
    assert nz % zb == 0
    rows = nsub * bq

    def cur(part):
        return pl.BlockSpec((zb, rows, w), lambda z, i: (z, i, col // w + part))

    def prev(part):
        return pl.BlockSpec((zb, bq, w), lambda z, i: (z, jnp.maximum(i * nsub - 1, 0), col // w + part))

    return pl.pallas_call(
        functools.partial(_dil_kernel, scale=DIL_HEAD_DIM ** -0.5),
        out_shape=(jax.ShapeDtypeStruct((nz, lc, w), BF16),
                   jax.ShapeDtypeStruct((nz, lc, V7X_LANES), F32)),
        grid=(nz // zb, lc // rows),
        in_specs=[cur(0), prev(1), cur(1), prev(2), cur(2)],
        out_specs=(pl.BlockSpec((zb, rows, w), lambda z, i: (z, i, 0)),
                   pl.BlockSpec((zb, rows, V7X_LANES), lambda z, i: (z, i, 0))),
        compiler_params=_params("parallel", "arbitrary"),
        name="dilated_attention",
    )(qkv, qkv, qkv, qkv, qkv)


def _residue_block(tm, seq, dilation, width):
    tiles = seq // tm
    return pl.BlockSpec((1, dilation, tm // dilation, width), lambda i, *_: (i // tiles, 0, i % tiles, 0))


def _emit_by_residue(x, strip_ref, dst_refs):
    tm, w = x.shape
    strips = w // V7X_LANES
    for s in range(strips):
        strip_ref[s] = x[:, s * V7X_LANES:(s + 1) * V7X_LANES]
    for dst in dst_refs:
        d, n = dst.shape[1], dst.shape[2]
        for c in range(d):
            for s in range(strips):
                dst[0, c, :, s * V7X_LANES:(s + 1) * V7X_LANES] = strip_ref[s, pl.ds(c, n, stride=d), :].astype(dst.dtype)


def _dilmix_kernel(o0_ref, l0_ref, o1_ref, l1_ref, o2_ref, l2_ref, out_ref, strip_ref, lse_ref):
    tm = out_ref.shape[0]
    heads = DIL_WIDTH // DIL_HEAD_DIM
    for gi, (o_ref, l_ref) in enumerate(((o1_ref, l1_ref), (o2_ref, l2_ref))):
        d = o_ref.shape[1]
        n = tm // d
        for c in range(d):
            lse_ref[gi, pl.ds(c, n, stride=d), :] = l_ref[0, c]
            for j in range(heads):
                strip_ref[gi, j, pl.ds(c, n, stride=d), :] = (
                    o_ref[0, c, :, j * DIL_HEAD_DIM:(j + 1) * DIL_HEAD_DIM].astype(F32))
    ls = [l0_ref[...], lse_ref[0], lse_ref[1]]
    mx = jnp.maximum(jnp.maximum(ls[0], ls[1]), ls[2])
    es = [jnp.exp(l - mx) for l in ls]
    inv = 1.0 / (es[0] + es[1] + es[2])
    ws = [e * inv for e in es]
    for j in range(heads):
        sl = slice(j * DIL_HEAD_DIM, (j + 1) * DIL_HEAD_DIM)
        acc = (ws[0][:, j:j + 1] * o0_ref[:, sl].astype(F32) + ws[1][:, j:j + 1] * strip_ref[0, j]
               + ws[2][:, j:j + 1] * strip_ref[1, j])
        out_ref[:, sl] = acc.astype(out_ref.dtype)


def _dilated_mixture(o0, l0, o1, l1, o2, l2, seq):
    t = o0.shape[0]
    tm = min(ROW_TM, t)
    heads = DIL_WIDTH // DIL_HEAD_DIM
    assert DIL_HEAD_DIM == V7X_LANES
    o_spec = pl.BlockSpec((tm, DIL_WIDTH), lambda i: (i, 0))
    l_spec = pl.BlockSpec((tm, V7X_LANES), lambda i: (i, 0))
    res = [spec for o in (o1, o2) for spec in (_residue_block(tm, seq, o.shape[1], DIL_WIDTH),
                                               _residue_block(tm, seq, o.shape[1], V7X_LANES))]
    return pl.pallas_call(
        _dilmix_kernel,
        out_shape=jax.ShapeDtypeStruct((t, DIL_WIDTH), BF16),
        grid=(t // tm,),
        in_specs=[o_spec, l_spec] + res,
        out_specs=o_spec,
        scratch_shapes=[pltpu.VMEM((2, heads, tm, DIL_HEAD_DIM), F32), pltpu.VMEM((2, tm, V7X_LANES), F32)],
        compiler_params=_params("parallel"),
        name="dilated_mix",
    )(o0, l0, o1, l1, o2, l2)


def _merge_kernel(ya_ref, yb_ref, yc_ref, ua_ref, ub_ref, uc_ref, ga_ref, gb_ref, gc_ref, bg_ref, o_ref):
    acc = None
    for i, (y_ref, u_ref, g_ref) in enumerate(((ya_ref, ua_ref, ga_ref), (yb_ref, ub_ref, gb_ref),
                                               (yc_ref, uc_ref, gc_ref))):
        gate = jax.nn.sigmoid(g_ref[...].astype(F32) + bg_ref[i:i + 1, :])
        term = gate * _dot(y_ref[...], u_ref[...])
        acc = term if acc is None else acc + term
    o_ref[...] = acc.astype(o_ref.dtype)


def _merge_up(y_ssd, y_sb, y_dil, layer, u_ssd, u_sb, u_dil, proj2d, b_gate):
    t = y_ssd.shape[0]
    tm, tn = min(MERGE_TM, t), MERGE_TN

    def rows(a):
        return pl.BlockSpec((tm, a.shape[1]), lambda i, j: (i, 0))

    def cols(a):
        return pl.BlockSpec((None, a.shape[1], tn), lambda i, j: (layer, 0, j))

    def gate(k):
        return pl.BlockSpec((tm, tn), lambda i, j: (i, (PB_GATE + k * D_MODEL) // tn + j))

    return pl.pallas_call(
        _merge_kernel,
        out_shape=jax.ShapeDtypeStruct((t, D_MODEL), BF16),
        grid=(t // tm, D_MODEL // tn),
        in_specs=[rows(y_ssd), rows(y_sb), rows(y_dil), cols(u_ssd), cols(u_sb), cols(u_dil),
                  gate(0), gate(1), gate(2), pl.BlockSpec((N_BRANCH, tn), lambda i, j: (0, j))],
        out_specs=pl.BlockSpec((tm, tn), lambda i, j: (i, j)),
        compiler_params=_params("parallel", "parallel"),
        name="merge_up",
    )(y_ssd, y_sb, y_dil, u_ssd, u_sb, u_dil, proj2d, proj2d, proj2d, b_gate)


def _outln_kernel(m_ref, w_ref, h_ref, g_ref, b_ref, rwh_ref, rwl_ref, rb_ref, of_ref, op_ref, meta_ref, wts_ref,
                  cnt_ref, *, alpha):
    @pl.when(pl.program_id(0) == 0)
    def _():
        cnt_ref[...] = jnp.zeros_like(cnt_ref)

    y = alpha * h_ref[...] + _dot(m_ref[...], w_ref[...])
    out = _layer_norm(y, g_ref[...], b_ref[...])
    of_ref[...] = out
    op_ref[...] = _pack_halves(out)
    e1, e2, w1, w2 = _route(out, rwh_ref[...], rwl_ref[...], rb_ref[...])
    r1, r2 = _expert_ranks(e1, e2, cnt_ref)
    for row, v in enumerate((e1, e2, r1, r2)):
        meta_ref[row:row + 1, :] = v
    wts_ref[0:1, :] = w1
    wts_ref[1:2, :] = w2


def _out_ln(merged, layer, w_out, h, g, b, router_w, router_bias, alpha):
    t, d = h.shape
    tm = min(ROW_TM, t)
    row = pl.BlockSpec((tm, d), lambda i: (i, 0))
    vec = pl.BlockSpec((1, d), lambda i: (0, 0))
    rw = jnp.pad(router_w.astype(F32), ((0, 0), (0, V7X_LANES - N_EXPERTS)))
    rw_hi = rw.astype(BF16)
    rw_lo = (rw - rw_hi.astype(F32)).astype(BF16)
    rw_spec = pl.BlockSpec((d, V7X_LANES), lambda i: (0, 0))
    return pl.pallas_call(
        functools.partial(_outln_kernel, alpha=alpha),
        out_shape=(jax.ShapeDtypeStruct((t, d), F32), jax.ShapeDtypeStruct((t, d // 2), jnp.uint32),
                   jax.ShapeDtypeStruct((4, t), jnp.int32), jax.ShapeDtypeStruct((2, t), F32),
                   jax.ShapeDtypeStruct((N_EXPERTS, V7X_LANES), F32)),
        grid=(t // tm,),
        in_specs=[row, pl.BlockSpec((None, d, d), lambda i: (layer, 0, 0)), row, vec, vec,
                  rw_spec, rw_spec, pl.BlockSpec((N_EXPERTS, 1), lambda i: (0, 0))],
        out_specs=(row, pl.BlockSpec((tm, d // 2), lambda i: (i, 0)),
                   pl.BlockSpec((4, tm), lambda i: (0, i)), pl.BlockSpec((2, tm), lambda i: (0, i)),
                   pl.BlockSpec((N_EXPERTS, V7X_LANES), lambda i: (0, 0))),
        compiler_params=_params("arbitrary"),
        name="out_ln",
    )(merged, w_out, h, g.reshape(1, d), b.reshape(1, d),
      rw_hi, rw_lo, router_bias.astype(F32).reshape(N_EXPERTS, 1))


def _route(h, rw_hi, rw_lo, rb):
    h_hi = h.astype(BF16)
    h_lo = (h - h_hi.astype(F32)).astype(BF16)
    logits = _dot(h_hi, rw_hi) + (_dot(h_lo, rw_hi) + _dot(h_hi, rw_lo))
    logits = logits.T[0:N_EXPERTS, :]
    mx = jnp.max(logits, axis=0, keepdims=True)
    ex = jnp.exp(logits - mx)
    aff = ex / jnp.sum(ex, axis=0, keepdims=True)
    sel = aff + rb
    s = [sel[e:e + 1, :] for e in range(N_EXPERTS)]
    a = [aff[e:e + 1, :] for e in range(N_EXPERTS)]
    n = EXPERTS_PER_GROUP
    best_score, best = None, None
    for g in range(N_EXPERT_GROUPS):
        grp = s[g * n:(g + 1) * n]
        score = None
        for i in range(n):
            for j in range(i + 1, n):
                pair = grp[i] + grp[j]
                score = pair if score is None else jnp.maximum(score, pair)
        if best is None:
            best_score, best = score, jnp.zeros_like(score, dtype=jnp.int32)
        else:
            take = score > best_score
            best_score = jnp.where(take, score, best_score)
            best = jnp.where(take, g, best)
    sv, av = [], []
    for r in range(n):
        sr, ar = s[r], a[r]
        for g in range(1, N_EXPERT_GROUPS):
            sr = jnp.where(best == g, s[g * n + r], sr)
            ar = jnp.where(best == g, a[g * n + r], ar)
        sv.append(sr)
        av.append(ar)

    def arg_top(vals, skip):
        top_v, top_i, top_a = None, None, None
        for r in range(n):
            v = vals[r] if skip is None else jnp.where(skip == r, -jnp.inf, vals[r])
            if top_v is None:
                top_v, top_i, top_a = v, jnp.zeros_like(best), av[0]
            else:
                take = v > top_v
                top_v = jnp.where(take, v, top_v)
                top_i = jnp.where(take, r, top_i)
                top_a = jnp.where(take, av[r], top_a)
        return top_i, top_a

    i1, a1 = arg_top(sv, None)
    i2, a2 = arg_top(sv, i1)
    tot = a1 + a2
    return best * n + i1, best * n + i2, a1 / tot, a2 / tot


def _expert_ranks(e1, e2, count_ref):
    tm = e1.shape[1]
    expert = lax.broadcasted_iota(jnp.int32, (N_EXPERTS, tm), 0)
    hit1, hit2 = expert == e1, expert == e2
    both = jnp.where(hit1, 1.0, 0.0) + jnp.where(hit2, 1.0, 0.0)
    earlier = lax.broadcasted_iota(jnp.int32, (tm, tm), 0) < lax.broadcasted_iota(jnp.int32, (tm, tm), 1)
    before = _dot(both.astype(BF16), jnp.where(earlier, 1.0, 0.0).astype(BF16)) + count_ref[:, 0:1]
    r1 = jnp.sum(jnp.where(hit1, before, 0.0), axis=0, keepdims=True)
    r2 = jnp.sum(jnp.where(hit2, before, 0.0), axis=0, keepdims=True)
    count_ref[...] = count_ref[...] + jnp.sum(both, axis=1, keepdims=True)
    return r1.astype(jnp.int32), r2.astype(jnp.int32)


def _moe_kernel(te_ref, nu_ref, fill_ref, next_ref, pos_ref, x_hbm, wg_hbm, wu_hbm, wd_hbm, o_ref,
                xbuf, sem, src_ref, stage_g, stage_u, stage_d, wbf_g, wbf_u, wbf_d, wsem, *, layer):
    i = pl.program_id(0)
    n_used = nu_ref[0]
    tm = xbuf.shape[1]
    n_tokens = pos_ref.shape[0] // 2
    w_hbm = (wg_hbm, wu_hbm, wd_hbm)
    stage = (stage_g, stage_u, stage_d)
    wbf = (wbf_g, wbf_u, wbf_d)

    def weight_copy(which, expert):
        return pltpu.make_async_copy(w_hbm[which].at[layer, expert], stage[which], wsem.at[which])

    @pl.when(i == 0)
    def _():
        for which in range(3):
            weight_copy(which, te_ref[0]).start()

    @pl.when((i < n_used) & ((i == 0) | (te_ref[jnp.maximum(i - 1, 0)] != te_ref[i])))
    def _():
        for which in range(3):
            weight_copy(which, te_ref[i]).wait()
            wbf[which][...] = stage[which][...].astype(BF16)

        @pl.when(next_ref[i] >= 0)
        def _():
            for which in range(3):
                weight_copy(which, next_ref[i]).start()

    @pl.when(i == 0)
    def _():
        def pad_expert(e, carry):
            def pad_slot(s, c):
                src_ref[s] = 0
                return c

            return lax.fori_loop(fill_ref[e], fill_ref[N_EXPERTS + e], pad_slot, carry)

        lax.fori_loop(0, N_EXPERTS, pad_expert, 0)
        for k in range(2):
            def place(tok, carry):
                src_ref[pos_ref[k * n_tokens + tok]] = tok
                return carry

            lax.fori_loop(0, n_tokens, place, 0, unroll=8)

    def fetch(tile):
        slot = tile % 2
        _start_row_gather(src_ref, tile * tm, x_hbm, xbuf.at[slot], sem.at[slot])

    @pl.when(i == 0)
    def _():
        fetch(i)

    @pl.when(i + 1 < n_used)
    def _():
        fetch(i + 1)

    @pl.when(i < n_used)
    def _():
        slot = i % 2
        _wait_row_gather(x_hbm, xbuf.at[slot], sem.at[slot])
        x = _unpack_halves(xbuf[slot]).astype(BF16)
        act = _silu(_dot(x, wbf_g[...])) * _dot(x, wbf_u[...])
        o_ref[...] = _pack_halves(_dot(act.astype(BF16), wbf_d[...]))

    @pl.when(i >= n_used)
    def _():
        o_ref[...] = jnp.zeros_like(o_ref)


def _route_plan(meta, counts, tm):
    t = meta.shape[1]
    padded = (counts + tm - 1) // tm * tm
    ends = jnp.cumsum(padded)
    starts = ends - padded
    one_hot = meta[0:2, :, None] == jnp.arange(N_EXPERTS, dtype=jnp.int32)
    pos = jnp.sum(jnp.where(one_hot, starts, 0), axis=-1) + meta[2:4]
    n_tiles = (2 * t + N_EXPERTS * tm) // tm
    n_used = (ends[-1] // tm).astype(jnp.int32)
    tile_start = jnp.minimum(jnp.arange(n_tiles, dtype=jnp.int32), n_used - 1) * tm
    owner = jnp.sum((ends[None, :] <= tile_start[:, None]).astype(jnp.int32), axis=1)
    owner = jnp.minimum(owner, N_EXPERTS - 1)
    fill = jnp.concatenate([starts + counts, ends]).astype(jnp.int32)
    expert = jnp.arange(N_EXPERTS, dtype=jnp.int32)
    later = (expert[None, :] > owner[:, None]) & (counts[None, :] > 0)
    next_expert = jnp.where(jnp.any(later, axis=1), jnp.argmax(later, axis=1).astype(jnp.int32), -1)
    return pos.reshape(-1), owner, n_used.reshape(1), fill, next_expert


def _moe_experts(x_packed, pos, tile_expert, n_used, fill, next_expert, layer, w_gate, w_up, w_down):
    n_tiles = tile_expert.shape[0]
    dp = x_packed.shape[1]
    tm = MOE_TM
    d, de = w_gate.shape[2], w_gate.shape[3]
    any_space = pl.BlockSpec(memory_space=pl.ANY)
    return pl.pallas_call(
        functools.partial(_moe_kernel, layer=layer),
        out_shape=jax.ShapeDtypeStruct((n_tiles * tm, dp), jnp.uint32),
        grid_spec=pltpu.PrefetchScalarGridSpec(
            num_scalar_prefetch=5,
            grid=(n_tiles,),
            in_specs=[any_space] * 4,
            out_specs=pl.BlockSpec((tm, dp), lambda i, *_: (i, 0)),
            scratch_shapes=[pltpu.VMEM((2, tm, dp), jnp.uint32), pltpu.SemaphoreType.DMA((2,)),
                            pltpu.SMEM((n_tiles * tm,), jnp.int32),
                            pltpu.VMEM((d, de), F32), pltpu.VMEM((d, de), F32), pltpu.VMEM((de, d), F32),
                            pltpu.VMEM((d, de), BF16), pltpu.VMEM((d, de), BF16), pltpu.VMEM((de, d), BF16),
                            pltpu.SemaphoreType.DMA((3,))]),
        compiler_params=pltpu.CompilerParams(dimension_semantics=("arbitrary",), vmem_limit_bytes=MOE_VMEM_LIMIT),
        name="moe_experts",
    )(tile_expert, n_used, fill, next_expert, pos, x_packed, w_gate, w_up, w_down)


def _ln2_kernel(pos_ref, h_ref, y_hbm, w_ref, g_ref, b_ref, of_ref, *rest, alpha, n_views):
    views, (rbuf, sem), strips = rest[:n_views], rest[n_views:n_views + 2], rest[n_views + 2:]
    i = pl.program_id(0)
    n_tiles = pl.num_programs(0)
    tm = h_ref.shape[0]

    def fetch(tile):
        slot = tile % 2
        for k in range(2):
            _start_row_gather(pos_ref, (k * n_tiles + tile) * tm, y_hbm, rbuf.at[slot, k], sem.at[slot, k])

    @pl.when(i == 0)
    def _():
        fetch(i)

    @pl.when(i + 1 < n_tiles)
    def _():
        fetch(i + 1)

    slot = i % 2
    y = alpha * h_ref[...]
    w = w_ref[...]
    for k in range(2):
        _wait_row_gather(y_hbm, rbuf.at[slot, k], sem.at[slot, k])
        y = y + w[:, k:k + 1] * _unpack_halves(rbuf[slot, k])
    out = _layer_norm(y, g_ref[...], b_ref[...])
    of_ref[...] = out
    if views:
        views[0][...] = out.astype(BF16)
        _emit_by_residue(out, strips[0], views[1:])


def _view_shapes_specs(bsz, seq, d, tm):
    shapes = [jax.ShapeDtypeStruct((bsz * seq, d), BF16)]
    specs = [pl.BlockSpec((tm, d), lambda i, *_: (i, 0))]
    for _, r in DIL_PAIRS[1:]:
        shapes.append(jax.ShapeDtypeStruct((bsz, r, seq // r, d), BF16))
        specs.append(_residue_block(tm, seq, r, d))
    return shapes, specs


def _combine_ln(h, y_packed, pos_by_choice, wts, g, b, alpha, bsz, seq, emit_views):
    t, d = h.shape
    dp = y_packed.shape[1]
    tm = min(ROW_TM, t)
    row = pl.BlockSpec((tm, d), lambda i, pos: (i, 0))
    vec = pl.BlockSpec((1, d), lambda i, pos: (0, 0))
    v_shapes, v_specs = _view_shapes_specs(bsz, seq, d, tm) if emit_views else ([], [])
    scratch = [pltpu.VMEM((2, 2, tm, dp), jnp.uint32), pltpu.SemaphoreType.DMA((2, 2))]
    if emit_views:
        scratch.append(pltpu.VMEM((d // V7X_LANES, tm, V7X_LANES), F32))
    out = pl.pallas_call(
        functools.partial(_ln2_kernel, alpha=alpha, n_views=len(v_shapes)),
        out_shape=[jax.ShapeDtypeStruct((t, d), F32)] + v_shapes,
        grid_spec=pltpu.PrefetchScalarGridSpec(
            num_scalar_prefetch=1,
            grid=(t // tm,),
            in_specs=[row, pl.BlockSpec(memory_space=pl.ANY), pl.BlockSpec((tm, 2), lambda i, pos: (i, 0)), vec, vec],
            out_specs=[row] + v_specs,
            scratch_shapes=scratch),
        compiler_params=_params("arbitrary"),
        name="combine_ln",
    )(pos_by_choice, h, y_packed, wts, g.reshape(1, d), b.reshape(1, d))
    return out[0], tuple(out[1:])


def _views_kernel(x_ref, *rest):
    views, strip_ref = rest[:-1], rest[-1]
    x = x_ref[...]
    views[0][...] = x.astype(BF16)
    _emit_by_residue(x, strip_ref, views[1:])


def _bf16_views(x2d, bsz, seq):
    t, d = x2d.shape
    tm = min(ROW_TM, t)
    v_shapes, v_specs = _view_shapes_specs(bsz, seq, d, tm)
    return tuple(pl.pallas_call(
        _views_kernel,
        out_shape=v_shapes,
        grid=(t // tm,),
        in_specs=[pl.BlockSpec((tm, d), lambda i: (i, 0))],
        out_specs=v_specs,
        scratch_shapes=[pltpu.VMEM((d // V7X_LANES, tm, V7X_LANES), F32)],
        compiler_params=_params("parallel"),
        name="bf16_views",
    )(x2d))


def _layer(h, views, bsz, seq, alpha, layer, last, w_a, w_b, w_dt, b_gate, conv_w, conv_b, dt_bias, a_log, d_skip,
           ssm_norm_g, w_up_ssd, w_up_sb, w_up_dil, w_out, ln1_g, ln1_b, router_w, router_bias,
           w_gate_e, w_up_e, w_down_e, ln2_g, ln2_b):
    t = bsz * seq
    h_b = views[0]
    dil_w = 3 * DIL_WIDTH
    w_dt = w_dt.reshape(D_MODEL, SSD_GROUPS, SSD_HEADS_PER_GROUP)
    w_dt = jnp.pad(w_dt, ((0, 0), (0, 0), (0, DT_LANES - SSD_HEADS_PER_GROUP))).reshape(D_MODEL, DT_WIDTH)
    proj_a = _matmul(h_b, w_a, BF16, MM_TM, MM_TN, layer=layer).reshape(bsz, seq, N_PROJ_A)
    proj2d = _matmul(h_b, w_b, BF16, MM_TM, MM_TN, layer=layer,
                     col_ranges=((WB_SB, WB_DIL + dil_w), (WB_GATE, WB_END)))
    dt_raw = _matmul(h_b, w_dt, F32, MM_TM, DT_WIDTH)
    proj = proj2d.reshape(bsz, seq, N_PROJ_B)

    y_ssd = _ssd(proj_a, dt_raw.reshape(bsz, seq, DT_WIDTH), conv_w, conv_b, dt_bias, a_log, d_skip, ssm_norm_g)
    y_sb = _stick_breaking(proj)

    assert DIL_PAIRS[0][1] == 1 and all(w // d == DIL_BLOCK for w, d in DIL_PAIRS)
    o, lse = _band_attention(proj, PB_DIL)
    dil = [o.reshape(t, DIL_WIDTH), lse.reshape(t, V7X_LANES)]
    for g, (_, r) in enumerate(DIL_PAIRS[1:], start=1):
        proj_g = _matmul(views[g].reshape(t, D_MODEL), w_b, BF16, MM_TM, MM_TN, layer=layer,
                         col_ranges=((WB_DIL + g * dil_w, WB_DIL + (g + 1) * dil_w),))
        o, lse = _band_attention(proj_g.reshape(bsz * r, seq // r, dil_w), 0)
        dil += [o.reshape(bsz, r, seq // r, DIL_WIDTH), lse.reshape(bsz, r, seq // r, V7X_LANES)]
    y_dil = _dilated_mixture(*dil, seq)

    merged = _merge_up(y_ssd.reshape(t, SSD_INNER), y_sb.reshape(t, SB_WIDTH), y_dil, layer,
                       w_up_ssd, w_up_sb, w_up_dil, proj2d, b_gate.astype(F32).reshape(N_BRANCH, D_MODEL))
    h1, h1_p, meta, wts, counts = _out_ln(merged, layer, w_out, h, ln1_g, ln1_b, router_w, router_bias, alpha)

    pos, tile_expert, n_used, fill, next_expert = _route_plan(meta, counts[:, 0].astype(jnp.int32), MOE_TM)
    y_p = _moe_experts(h1_p, pos, tile_expert, n_used, fill, next_expert, layer, w_gate_e, w_up_e, w_down_e)
    return _combine_ln(h1, y_p, pos, wts.T, ln2_g, ln2_b, alpha, bsz, seq, emit_views=not last)


def kernel(x, w_in, b_gate, conv_w, conv_b, dt_bias, a_log, d_skip, ssm_norm_g, w_up_ssd, w_up_sb, w_up_dil,
           w_out, ln1_g, ln1_b, router_w, router_bias, w_gate_e, w_up_e, w_down_e, ln2_g, ln2_b):
    bsz, seq, d = x.shape
    depth = w_in.shape[0]
    alpha = (2 * depth) ** 0.25
    h = x.reshape(bsz * seq, d)
    views = _bf16_views(h, bsz, seq)
    assert w_in.shape[2] == _IN_END
    w_t = jnp.swapaxes(w_in, 1, 2)
    w_a = _cast_cols(w_t, 0, N_PROJ_A)
    w_b = _cast_cols(w_t, _IN_DT1, WB_END)
    w_dt = _cast_cols(w_t, _IN_DT0, V7X_LANES)[:, :, :SSD_HEADS]
    w_up_ssd, w_up_sb, w_up_dil, w_out = (w.astype(BF16) for w in (w_up_ssd, w_up_sb, w_up_dil, w_out))
    for l in range(depth):
        h, views = _layer(h, views, bsz, seq, alpha, l, l == depth - 1, w_a, w_b, w_dt[l],
                          b_gate[l], conv_w[l], conv_b[l], dt_bias[l], a_log[l], d_skip[l], ssm_norm_g[l],
                          w_up_ssd, w_up_sb, w_up_dil, w_out, ln1_g[l], ln1_b[l], router_w, router_bias,
                          w_gate_e, w_up_e, w_down_e, ln2_g[l], ln2_b[l])
    return h.reshape(bsz, seq, d)
```
